```python
import math
import jax, jax.numpy as jnp
from jax import lax
import numpy as np

D_MODEL = 1024
BATCH = 2
SEQ = 8192
DEPTH = 4
DEC_BATCH = 128
DEC_SEQ = 1
PAST_LEN = 2048
PAGE_SIZE = 128

N_A_LAYERS = DEPTH // 2
N_B_LAYERS = DEPTH - N_A_LAYERS
CONV_W = 3
N_HEADS = 8
HEAD_DIM = D_MODEL // N_HEADS // 2
QK_DIM = N_HEADS * 2 * HEAD_DIM
V_DIM = 2 * HEAD_DIM
VO_DIM = N_HEADS * V_DIM
ROT_DIM = HEAD_DIM // 4
ROPE_THETA = 500000.0
D_FF = -(-8 * D_MODEL // (3 * 256)) * 256
DEEPNORM_ALPHA = (2 * DEPTH) ** 0.25
DEEPNORM_BETA = (8 * DEPTH) ** -0.25
LN_EPS = 1e-5
Q_BLOCK = 128

kernel_name = "yoco_shortconv_diffattn_step"


def layer_norm(x, g, b):
    xf = x.astype(jnp.float32)
    mu = xf.mean(-1, keepdims=True)
    var = jnp.square(xf - mu).mean(-1, keepdims=True)
    return ((xf - mu) * lax.rsqrt(var + LN_EPS) * g.astype(jnp.float32) + b.astype(jnp.float32)).astype(x.dtype)


def rms_norm(x, g):
    xf = x.astype(jnp.float32)
    return (xf * lax.rsqrt(jnp.mean(xf * xf, -1, keepdims=True) + LN_EPS) * g.astype(jnp.float32)).astype(x.dtype)


def partial_rope(x, pos):
    half = ROT_DIM // 2
    inv = jnp.power(ROPE_THETA, -jnp.arange(0, ROT_DIM, 2, dtype=jnp.float32) / ROT_DIM)
    ang = pos.astype(jnp.float32)[:, None] * inv[None, :]
    cos = jnp.cos(ang)[None, :, None, None, :].astype(x.dtype)
    sin = jnp.sin(ang)[None, :, None, None, :].astype(x.dtype)
    x1 = x[..., :half]
    x2 = x[..., half:ROT_DIM]
    return jnp.concatenate([x1 * cos - x2 * sin, x2 * cos + x1 * sin, x[..., ROT_DIM:]], axis=-1)


def short_conv_mixer(x, conv_state, w_in, conv_w, w_out):
    T = x.shape[1]
    b, c, h = jnp.split(x @ w_in, 3, axis=-1)
    u = c * h
    buf = jnp.concatenate([conv_state.astype(u.dtype), u], axis=1)
    conv = sum(conv_w[j] * buf[:, j:j + T] for j in range(CONV_W))
    return (b * conv) @ w_out, buf[:, T:]


def swiglu(x, w_gate, w_up, w_down):
    return (jax.nn.silu(x @ w_gate) * (x @ w_up)) @ w_down


def diff_softmax_attention(q, k, v, q_pos, k_pos, lam):
    s = jnp.einsum('nqhcd,nkhcd->nhcqk', q, k).astype(jnp.float32) * (HEAD_DIM ** -0.5)
    mask = k_pos[None, :] <= q_pos[:, None]
    p = jax.nn.softmax(jnp.where(mask, s, -jnp.inf), axis=-1)
    a = (p[:, :, 0] - lam * p[:, :, 1]).astype(v.dtype)
    return jnp.einsum('nhqk,nkhe->nqhe', a, v)


def blocked_causal_diff_attention(q, k, v, lam):
    N, T = q.shape[:2]
    nblk = T // Q_BLOCK
    qb = q.reshape(N, nblk, Q_BLOCK, N_HEADS, 2, HEAD_DIM).swapaxes(0, 1)
    k_pos = jnp.arange(T)

    def one_block(args):
        q_blk, start = args
        return diff_softmax_attention(q_blk, k, v, start + jnp.arange(Q_BLOCK), k_pos, lam)

    out = lax.map(one_block, (qb, jnp.arange(nblk) * Q_BLOCK))
    return out.swapaxes(0, 1).reshape(N, T, N_HEADS, V_DIM)


def trunk(x, pos, conv_state, past_k, past_v,
          w_in, conv_w, w_mix_out, w_kv, w_q, w_o,
          lambda_q1, lambda_k1, lambda_q2, lambda_k2, subln_g,
          ln1_g, ln1_b, w_gate, w_up, w_down, ln2_g, ln2_b):
    N, T, _ = x.shape
    new_conv = []
    k_new = v_new = k_all = v_all = k_pos = None
    for l in range(DEPTH):
        if l < N_A_LAYERS:
            y, st = short_conv_mixer(x, conv_state[l], w_in[l], conv_w[l], w_mix_out[l])
            new_conv.append(st)
        else:
            if l == N_A_LAYERS:
                kv = x @ w_kv
                k_new = partial_rope(kv[..., :QK_DIM].reshape(N, T, N_HEADS, 2, HEAD_DIM), pos)
                v_new = kv[..., QK_DIM:].reshape(N, T, N_HEADS, V_DIM)
                if past_k is None:
                    k_all, v_all = k_new, v_new
                else:
                    k_all = jnp.concatenate([past_k.astype(k_new.dtype), k_new], axis=1)
                    v_all = jnp.concatenate([past_v.astype(v_new.dtype), v_new], axis=1)
                    k_pos = jnp.arange(k_all.shape[1])
            i = l - N_A_LAYERS
            lam_init = 0.8 - 0.6 * math.exp(-0.3 * l)
            lam = (jnp.exp(jnp.sum(lambda_q1[i].astype(jnp.float32) * lambda_k1[i].astype(jnp.float32)))
                   - jnp.exp(jnp.sum(lambda_q2[i].astype(jnp.float32) * lambda_k2[i].astype(jnp.float32)))
                   + lam_init)
            q = partial_rope((x @ w_q[i]).reshape(N, T, N_HEADS, 2, HEAD_DIM), pos)
            if past_k is None:
                o = blocked_causal_diff_attention(q, k_all, v_all, lam)
            else:
                o = diff_softmax_attention(q, k_all, v_all, pos, k_pos, lam)
            o = rms_norm(o, subln_g[i]) * (1.0 - lam_init)
            y = o.reshape(N, T, VO_DIM) @ w_o[i]
        x = layer_norm(DEEPNORM_ALPHA * x + y, ln1_g[l], ln1_b[l])
        x = layer_norm(DEEPNORM_ALPHA * x + swiglu(x, w_gate[l], w_up[l], w_down[l]), ln2_g[l], ln2_b[l])
    return x, k_new, v_new, jnp.stack(new_conv)


def setup_inputs(seed: int = 0) -> dict:
    key = jax.random.key(seed)
    ks = jax.random.split(key, 32)
    f32 = jnp.float32
    n_pages = PAST_LEN // PAGE_SIZE
    n_used = DEC_BATCH * n_pages
    n_pool = n_used + max(1, n_used // 4)
    nrm = lambda k, shape, s: jax.random.normal(k, shape, f32) * s

    x_prompt = nrm(ks[0], (BATCH, SEQ, D_MODEL), 1.0)
    x_sample = nrm(ks[1], (DEC_BATCH, DEC_SEQ, D_MODEL), 1.0)
    cache_k = nrm(ks[2], (n_pool, PAGE_SIZE, N_HEADS, 2, HEAD_DIM), 1.0)
    cache_v = nrm(ks[3], (n_pool, PAGE_SIZE, N_HEADS, V_DIM), DEEPNORM_BETA)
    state_conv = nrm(ks[4], (N_A_LAYERS, DEC_BATCH, CONV_W - 1, D_MODEL), 0.5)
    page_table = jax.random.permutation(ks[5], n_pool)[:n_used].reshape(DEC_BATCH, n_pages).astype(jnp.int32)

    sd = D_MODEL ** -0.5
    w_in = nrm(ks[6], (N_A_LAYERS, D_MODEL, 3 * D_MODEL), sd)
    w_in = w_in * jnp.concatenate([jnp.ones((2 * D_MODEL,), f32), jnp.full((D_MODEL,), DEEPNORM_BETA, f32)])
    conv_w = nrm(ks[7], (N_A_LAYERS, CONV_W, D_MODEL), CONV_W ** -0.5)
    w_mix_out = nrm(ks[8], (N_A_LAYERS, D_MODEL, D_MODEL), sd * DEEPNORM_BETA)
    w_kv = jnp.concatenate([nrm(ks[9], (D_MODEL, QK_DIM), sd),
                            nrm(ks[10], (D_MODEL, VO_DIM), sd * DEEPNORM_BETA)], axis=1)
    w_q = nrm(ks[11], (N_B_LAYERS, D_MODEL, QK_DIM), sd)
    w_o = nrm(ks[12], (N_B_LAYERS, VO_DIM, D_MODEL), VO_DIM ** -0.5 * DEEPNORM_BETA)
    lambda_q1 = nrm(ks[13], (N_B_LAYERS, HEAD_DIM), 0.1)
    lambda_k1 = nrm(ks[14], (N_B_LAYERS, HEAD_DIM), 0.1)
    lambda_q2 = nrm(ks[15], (N_B_LAYERS, HEAD_DIM), 0.1)
    lambda_k2 = nrm(ks[16], (N_B_LAYERS, HEAD_DIM), 0.1)
    subln_g = 1.0 + nrm(ks[17], (N_B_LAYERS, V_DIM), 0.02)
    ln1_g = 1.0 + nrm(ks[18], (DEPTH, D_MODEL), 0.02)
    ln1_b = nrm(ks[19], (DEPTH, D_MODEL), 0.02)
    w_gate = nrm(ks[20], (DEPTH, D_MODEL, D_FF), sd)
    w_up = nrm(ks[21], (DEPTH, D_MODEL, D_FF), sd * DEEPNORM_BETA)
    w_down = nrm(ks[22], (DEPTH, D_FF, D_MODEL), D_FF ** -0.5 * DEEPNORM_BETA)
    ln2_g = 1.0 + nrm(ks[23], (DEPTH, D_MODEL), 0.02)
    ln2_b = nrm(ks[24], (DEPTH, D_MODEL), 0.02)
    return {"x_prompt": x_prompt, "x_sample": x_sample, "cache_k": cache_k, "cache_v": cache_v,
            "state_conv": state_conv, "page_table": page_table,
            "w_in": w_in, "conv_w": conv_w, "w_mix_out": w_mix_out, "w_kv": w_kv,
            "w_q": w_q, "w_o": w_o, "lambda_q1": lambda_q1, "lambda_k1": lambda_k1,
            "lambda_q2": lambda_q2, "lambda_k2": lambda_k2, "subln_g": subln_g,
            "ln1_g": ln1_g, "ln1_b": ln1_b, "w_gate": w_gate, "w_up": w_up, "w_down": w_down,
            "ln2_g": ln2_g, "ln2_b": ln2_b}


def reference(x_prompt, x_sample, cache_k, cache_v, state_conv, page_table,
              w_in, conv_w, w_mix_out, w_kv, w_q, w_o,
              lambda_q1, lambda_k1, lambda_q2, lambda_k2, subln_g,
              ln1_g, ln1_b, w_gate, w_up, w_down, ln2_g, ln2_b):
    weights = (w_in, conv_w, w_mix_out, w_kv, w_q, w_o,
               lambda_q1, lambda_k1, lambda_q2, lambda_k2, subln_g,
               ln1_g, ln1_b, w_gate, w_up, w_down, ln2_g, ln2_b)
    n_p, t_p, _ = x_prompt.shape
    zero_conv = jnp.zeros((N_A_LAYERS, n_p, CONV_W - 1, D_MODEL), x_prompt.dtype)
    y_prompt, k_prompt, v_prompt, conv_prompt = trunk(
        x_prompt, jnp.arange(t_p), zero_conv, None, None, *weights)

    n_s, t_s, _ = x_sample.shape
    n_pages = page_table.shape[1]
    past_len = n_pages * cache_k.shape[1]
    past_k = cache_k[page_table].reshape(n_s, past_len, N_HEADS, 2, HEAD_DIM)
    past_v = cache_v[page_table].reshape(n_s, past_len, N_HEADS, V_DIM)
    y_sample, k_sample, v_sample, conv_sample = trunk(
        x_sample, past_len + jnp.arange(t_s), state_conv, past_k, past_v, *weights)
    return (y_prompt, y_sample, k_prompt, v_prompt, conv_prompt, k_sample, v_sample, conv_sample)
```

```python
import functools
import math

import jax
import jax.numpy as jnp
from jax import lax
from jax.experimental import pallas as pl
from jax.experimental.pallas import tpu as pltpu

D_MODEL = 1024
DEPTH = 4
N_A_LAYERS = DEPTH // 2
N_HEADS = 8
HEAD_DIM = 64
V_DIM = 2 * HEAD_DIM
QK_DIM = N_HEADS * 2 * HEAD_DIM
VO_DIM = N_HEADS * V_DIM
ROT_DIM = HEAD_DIM // 4
ROPE_THETA = 500000.0
D_FF = 2816
ALPHA = (2 * DEPTH) ** 0.25
LN_EPS = 1e-5
LANES = 128
SUBLANES = 8
NEG = -1e30
VMEM_LIMIT = 56 * 1024 * 1024

BF = jnp.bfloat16
F32 = jnp.float32


def _dot(a, b):
    return jnp.dot(a, b, preferred_element_type=F32)


def _dot_nt(a, b):
    return lax.dot_general(a, b, (((1,), (1,)), ((), ())), preferred_element_type=F32)


def _ln(x, g, b):
    mu = jnp.mean(x, axis=-1, keepdims=True)
    xc = x - mu
    var = jnp.mean(xc * xc, axis=-1, keepdims=True)
    return xc * lax.rsqrt(var + LN_EPS) * g + b


def _lam(lq1, lk1, lq2, lk2, i, lam_init):
    a = jnp.sum(lq1[i:i + 1, :] * lk1[i:i + 1, :], axis=-1, keepdims=True)
    b = jnp.sum(lq2[i:i + 1, :] * lk2[i:i + 1, :], axis=-1, keepdims=True)
    return jnp.exp(a) - jnp.exp(b) + lam_init


def _const_spec(shape):
    nd = len(shape)
    return pl.BlockSpec(shape, lambda *_: (0,) * nd, pipeline_mode=pl.Buffered(1))


def _params(sem):
    return pltpu.CompilerParams(dimension_semantics=sem, vmem_limit_bytes=VMEM_LIMIT)


def _mixer_kernel(x_ref, win_ref, cw_ref, a_ref, st_ref, carry_ref):
    @pl.when(pl.program_id(1) == 0)
    def _():
        carry_ref[...] = jnp.zeros_like(carry_ref)

    z = _dot(x_ref[0].astype(BF), win_ref[...])
    b = z[:, :D_MODEL]
    u = z[:, D_MODEL:2 * D_MODEL] * z[:, 2 * D_MODEL:]
    tm = u.shape[0]
    prev = carry_ref[...]
    row = lax.broadcasted_iota(jnp.int32, (SUBLANES, D_MODEL), 0)

    def shifted(k):
        us = pltpu.roll(u, k, 0)
        head = jnp.where(row < k, pltpu.roll(prev, k, 0), us[:SUBLANES])
        return jnp.concatenate([head, us[SUBLANES:]], axis=0)

    conv = cw_ref[0:1, :] * shifted(2) + cw_ref[1:2, :] * shifted(1) + cw_ref[2:3, :] * u
    a_ref[0] = (b * conv).astype(BF)
    last = u[tm - SUBLANES:]
    carry_ref[...] = last
    st_ref[0] = last


def _mixer_prompt(x, w_in, cw, tm=512):
    n, t, _ = x.shape
    return pl.pallas_call(
        _mixer_kernel,
        grid=(n, t // tm),
        in_specs=[pl.BlockSpec((1, tm, D_MODEL), lambda i, j: (i, j, 0)),
                  _const_spec((D_MODEL, 3 * D_MODEL)),
                  _const_spec((3, D_MODEL))],
        out_specs=[pl.BlockSpec((1, tm, D_MODEL), lambda i, j: (i, j, 0)),
                   pl.BlockSpec((1, SUBLANES, D_MODEL), lambda i, j: (i, 0, 0))],
        out_shape=[jax.ShapeDtypeStruct((n, t, D_MODEL), BF),
                   jax.ShapeDtypeStruct((n, SUBLANES, D_MODEL), F32)],
        scratch_shapes=[pltpu.VMEM((SUBLANES, D_MODEL), F32)],
        compiler_params=_params(("arbitrary", "arbitrary")),
        name="mixer_prompt",
    )(x, w_in, cw)


def _mixer_dec_kernel(x_ref, s0_ref, s1_ref, win_ref, cw_ref, a_ref, u_ref):
    z = _dot(x_ref[...].astype(BF), win_ref[...])
    b = z[:, :D_MODEL]
    u = z[:, D_MODEL:2 * D_MODEL] * z[:, 2 * D_MODEL:]
    conv = cw_ref[0:1, :] * s0_ref[...] + cw_ref[1:2, :] * s1_ref[...] + cw_ref[2:3, :] * u
    a_ref[...] = (b * conv).astype(BF)
    u_ref[...] = u


def _mixer_dec(x, s0, s1, w_in, cw):
    r = x.shape[0]
    row = pl.BlockSpec((r, D_MODEL), lambda i: (0, 0))
    return pl.pallas_call(
        _mixer_dec_kernel,
        grid=(1,),
        in_specs=[row, row, row, _const_spec((D_MODEL, 3 * D_MODEL)), _const_spec((3, D_MODEL))],
        out_specs=[row, row],
        out_shape=[jax.ShapeDtypeStruct((r, D_MODEL), BF), jax.ShapeDtypeStruct((r, D_MODEL), F32)],
        compiler_params=_params(("arbitrary",)),
        name="mixer_dec",
    )(x, s0, s1, w_in, cw)


def _post_kernel(x_ref, a_ref, w_ref, g1_ref, b1_ref, wg_ref, wu_ref, wd_ref, g2_ref, b2_ref,
                 o_ref, *, head_major):
    if head_major:
        a = jnp.concatenate([a_ref[0, h] for h in range(N_HEADS)], axis=1)
    else:
        a = a_ref[...]
    x1 = _ln(ALPHA * x_ref[...] + _dot(a, w_ref[...]), g1_ref[...], b1_ref[...])
    x1b = x1.astype(BF)
    hh = (jax.nn.silu(_dot(x1b, wg_ref[...])) * _dot(x1b, wu_ref[...])).astype(BF)
    o_ref[...] = _ln(ALPHA * x1 + _dot(hh, wd_ref[...]), g2_ref[...], b2_ref[...])


def _post(x, a, w, g1, b1, wg, wu, wd, g2, b2, tm, head_major=False):
    r = x.shape[0]
    row = pl.BlockSpec((tm, D_MODEL), lambda i: (i, 0))
    if head_major:
        nt = a.shape[2] // tm
        a_spec = pl.BlockSpec((1, N_HEADS, tm, V_DIM), lambda i: (i // nt, 0, i % nt, 0))
    else:
        a_spec = row
    vec = _const_spec((1, D_MODEL))
    return pl.pallas_call(
        functools.partial(_post_kernel, head_major=head_major),
        grid=(r // tm,),
        in_specs=[row, a_spec, _const_spec((D_MODEL, D_MODEL)), vec, vec,
                  _const_spec((D_MODEL, D_FF)), _const_spec((D_MODEL, D_FF)),
                  _const_spec((D_FF, D_MODEL)), vec, vec],
        out_specs=row,
        out_shape=jax.ShapeDtypeStruct((r, D_MODEL), F32),
        compiler_params=_params(("arbitrary",)),
        name="post",
    )(x, a, w, g1, b1, wg, wu, wd, g2, b2)


def _rope(z, c, sa, sb):
    parts = []
    for j in range(z.shape[1] // LANES):
        zj = z[:, j * LANES:(j + 1) * LANES]
        parts.append(zj * c + pltpu.roll(zj, LANES - ROT_DIM // 2, 1) * sa
                     + pltpu.roll(zj, ROT_DIM // 2, 1) * sb)
    return jnp.concatenate(parts, axis=1)


def _store_heads(ref, val, head_major):
    if head_major:
        for h in range(N_HEADS):
            ref[0, h] = val[:, h * V_DIM:(h + 1) * V_DIM]
    else:
        ref[...] = val


def _qkv_kernel(x_ref, w_ref, c_ref, sa_ref, sb_ref, *out_refs, with_kv, head_major):
    z = _dot(x_ref[...].astype(BF), w_ref[...])
    c, sa, sb = c_ref[...], sa_ref[...], sb_ref[...]
    if with_kv:
        k_ref, v_ref, kb_ref, vb_ref, q_ref = out_refs
        k = _rope(z[:, :QK_DIM], c, sa, sb)
        v = z[:, QK_DIM:QK_DIM + VO_DIM]
        k_ref[...] = k
        v_ref[...] = v
        _store_heads(kb_ref, k.astype(BF), head_major)
        _store_heads(vb_ref, v.astype(BF), head_major)
        zq = z[:, QK_DIM + VO_DIM:]
    else:
        (q_ref,) = out_refs
        zq = z
    q = _rope(zq, c, sa, sb) * (HEAD_DIM ** -0.5)
    _store_heads(q_ref, q.astype(BF), head_major)


def _qkv(x, w, tabs, tm, with_kv, n_batch):
    r = x.shape[0]
    wn = w.shape[1]
    head_major = n_batch > 0
    nt = (r // n_batch) // tm if head_major else r // tm
    row = pl.BlockSpec((tm, D_MODEL), lambda i: (i, 0))
    tab = pl.BlockSpec((tm, LANES), lambda i: (i % nt, 0))
    if head_major:
        hm_spec = pl.BlockSpec((1, N_HEADS, tm, V_DIM), lambda i: (i // nt, 0, i % nt, 0))
        hm_shape = jax.ShapeDtypeStruct((n_batch, N_HEADS, r // n_batch, V_DIM), BF)
    else:
        hm_spec = row
        hm_shape = jax.ShapeDtypeStruct((r, D_MODEL), BF)
    f32_shape = jax.ShapeDtypeStruct((r, D_MODEL), F32)
    if with_kv:
        out_specs = [row, row, hm_spec, hm_spec, hm_spec]
        out_shape = [f32_shape, f32_shape, hm_shape, hm_shape, hm_shape]
    else:
        out_specs = [hm_spec]
        out_shape = [hm_shape]
    return pl.pallas_call(
        functools.partial(_qkv_kernel, with_kv=with_kv, head_major=head_major),
        grid=(r // tm,),
        in_specs=[row, _const_spec((D_MODEL, wn)), tab, tab, tab],
        out_specs=out_specs,
        out_shape=out_shape,
        compiler_params=_params(("arbitrary",)),
        name="qkv",
    )(x, w, *tabs)


def _attn_kernel(q_ref, k_ref, v_ref, lq1_ref, lk1_ref, lq2_ref, lk2_ref, g_ref, o_ref,
                 q2_scr, m_scr, l_scr, acc_scr, *, layer_i, lam_init, tq):
    i = pl.program_id(1)
    j = pl.program_id(2)

    @pl.when(j == 0)
    def _init():
        lane = lax.broadcasted_iota(jnp.int32, (tq, V_DIM), 1)

        def body(h, carry):
            qh = q_ref[0, h]
            zero = jnp.zeros_like(qh)
            q2_scr[h, :tq, :] = jnp.where(lane < HEAD_DIM, qh, zero)
            q2_scr[h, tq:, :] = jnp.where(lane >= HEAD_DIM, qh, zero)
            return carry

        lax.fori_loop(0, N_HEADS, body, 0)
        m_scr[...] = jnp.full_like(m_scr, NEG)
        l_scr[...] = jnp.zeros_like(l_scr)
        acc_scr[...] = jnp.zeros_like(acc_scr)

    def step(masked):
        def body(h, carry):
            s = _dot_nt(q2_scr[h], k_ref[0, h])
            if masked:
                r = lax.broadcasted_iota(jnp.int32, s.shape, 0)
                c = lax.broadcasted_iota(jnp.int32, s.shape, 1)
                s = jnp.where(c > jnp.where(r >= tq, r - tq, r), NEG, s)
            m_prev = m_scr[h][:, :1]
            l_prev = l_scr[h][:, :1]
            m_new = jnp.maximum(m_prev, jnp.max(s, axis=-1, keepdims=True))
            alpha = jnp.exp(m_prev - m_new)
            p = jnp.exp(s - m_new)
            l_new = alpha * l_prev + jnp.sum(p, axis=-1, keepdims=True)
            acc_scr[h] = alpha * acc_scr[h] + _dot(p.astype(BF), v_ref[0, h])
            m_scr[h] = jnp.broadcast_to(m_new, (2 * tq, LANES))
            l_scr[h] = jnp.broadcast_to(l_new, (2 * tq, LANES))
            return carry

        lax.fori_loop(0, N_HEADS, body, 0)

    @pl.when(j < i)
    def _():
        step(False)

    @pl.when(j == i)
    def _():
        step(True)
        lam = _lam(lq1_ref, lk1_ref, lq2_ref, lk2_ref, layer_i, lam_init)
        g = g_ref[layer_i:layer_i + 1, :] * (1.0 - lam_init)

        def body(h, carry):
            acc = acc_scr[h]
            l = l_scr[h][:, :1]
            o = acc[:tq] / l[:tq] - lam * (acc[tq:] / l[tq:])
            ms = jnp.mean(o * o, axis=-1, keepdims=True)
            o_ref[0, h] = (o * lax.rsqrt(ms + LN_EPS) * g).astype(BF)
            return carry

        lax.fori_loop(0, N_HEADS, body, 0)


def _attn_prompt(q, k, v, lams, g, layer_i, lam_init, tq=512):
    n, h, t, _ = q.shape
    nq = t // tq
    qspec = pl.BlockSpec((1, h, tq, V_DIM), lambda b, i, j: (b, 0, i, 0))
    kvspec = pl.BlockSpec((1, h, tq, V_DIM), lambda b, i, j: (b, 0, jnp.minimum(i, j), 0))
    lspec = _const_spec(lams[0].shape)
    return pl.pallas_call(
        functools.partial(_attn_kernel, layer_i=layer_i, lam_init=lam_init, tq=tq),
        grid=(n, nq, nq),
        in_specs=[qspec, kvspec, kvspec, lspec, lspec, lspec, lspec, _const_spec(g.shape)],
        out_specs=qspec,
        out_shape=jax.ShapeDtypeStruct((n, h, t, V_DIM), BF),
        scratch_shapes=[pltpu.VMEM((h, 2 * tq, V_DIM), BF),
                        pltpu.VMEM((h, 2 * tq, LANES), F32),
                        pltpu.VMEM((h, 2 * tq, LANES), F32),
                        pltpu.VMEM((h, 2 * tq, V_DIM), F32)],
        compiler_params=_params(("arbitrary", "arbitrary", "arbitrary")),
        name="attn_prompt",
    )(q, k, v, *lams, g)


def _attn_dec_kernel(pt_ref, q_ref, kn_ref, vn_ref, lq1_ref, lk1_ref, lq2_ref, lk2_ref, g_ref,
                     *rest, layer_i, lam_init, n_pages):
    del pt_ref
    k_refs = rest[:n_pages]
    v_refs = rest[n_pages:2 * n_pages]
    o_ref = rest[2 * n_pages]
    nm = 2 * N_HEADS
    lam = _lam(lq1_ref, lk1_ref, lq2_ref, lk2_ref, layer_i, lam_init)

    grp = lax.shift_right_logical(lax.broadcasted_iota(jnp.int32, (nm, QK_DIM), 1), 6)
    r = lax.broadcasted_iota(jnp.int32, (nm, QK_DIM), 0)
    sel = r == (grp & 1) * N_HEADS + lax.shift_right_logical(grp, 1)
    qf = jnp.broadcast_to(q_ref[0].astype(F32), (nm, QK_DIM))
    qbd = jnp.where(sel, qf, 0.0)
    qbd_b = qbd.astype(BF)

    s = jnp.concatenate([_dot_nt(qbd_b, k_refs[p][0].astype(BF)) for p in range(n_pages)],
                        axis=1)
    s_new = jnp.sum(qbd * kn_ref[0].astype(F32), axis=-1, keepdims=True)
    m = jnp.maximum(jnp.max(s, axis=-1, keepdims=True), s_new)
    p = jnp.exp(s - m)
    p_new = jnp.exp(s_new - m)
    l = jnp.sum(p, axis=-1, keepdims=True) + p_new
    pn = p / l
    pn_new = p_new / l
    a = pn[:N_HEADS] - lam * pn[N_HEADS:]
    a_new = pn_new[:N_HEADS] - lam * pn_new[N_HEADS:]
    a2 = jnp.concatenate([a, a], axis=0).astype(BF)
    o = jnp.zeros((nm, VO_DIM), F32)
    for pg in range(n_pages):
        o = o + _dot(a2[:, pg * LANES:(pg + 1) * LANES], v_refs[pg][0].astype(BF))
    o = o[:N_HEADS] + a_new.astype(BF).astype(F32) * vn_ref[0].astype(F32)

    hl = lax.shift_right_logical(lax.broadcasted_iota(jnp.int32, (N_HEADS, VO_DIM), 1), 7)
    own = hl == lax.broadcasted_iota(jnp.int32, (N_HEADS, VO_DIM), 0)
    om = jnp.where(own, o, 0.0)
    ms = jnp.sum(om * om, axis=-1, keepdims=True) * (1.0 / V_DIM)
    orow = jnp.sum(om * lax.rsqrt(ms + LN_EPS), axis=0, keepdims=True)
    g = g_ref[layer_i:layer_i + 1, :]
    gt = jnp.concatenate([g] * N_HEADS, axis=1)
    o_ref[0] = (orow * gt * (1.0 - lam_init)).astype(BF)


def _attn_dec(pt_flat, q, kn, vn, cache_k, cache_v, lams, g, layer_i, lam_init, n_pages):
    ns = q.shape[0]
    page = cache_k.shape[1]
    tok = pl.BlockSpec((1, 1, D_MODEL), lambda s, pt: (s, 0, 0))

    def page_spec(pg):
        return pl.BlockSpec((1, page, D_MODEL), lambda s, pt: (pt[s * n_pages + pg], 0, 0))

    lspec = _const_spec(lams[0].shape)
    grid_spec = pltpu.PrefetchScalarGridSpec(
        num_scalar_prefetch=1,
        grid=(ns,),
        in_specs=[tok, tok, tok, lspec, lspec, lspec, lspec, _const_spec(g.shape)]
                 + [page_spec(pg) for pg in range(n_pages)] * 2,
        out_specs=tok,
    )
    return pl.pallas_call(
        functools.partial(_attn_dec_kernel, layer_i=layer_i, lam_init=lam_init, n_pages=n_pages),
        grid_spec=grid_spec,
        out_shape=jax.ShapeDtypeStruct((ns, 1, D_MODEL), BF),
        compiler_params=_params(("arbitrary",)),
        name="attn_dec",
    )(pt_flat, q, kn, vn, *lams, g, *([cache_k] * n_pages), *([cache_v] * n_pages))


def _rope_tables(pos):
    half = ROT_DIM // 2
    inv = jnp.power(ROPE_THETA, -jnp.arange(0, ROT_DIM, 2, dtype=F32) / ROT_DIM)
    ang = pos.astype(F32)[:, None] * inv[None, :]
    cos, sin = jnp.cos(ang), jnp.sin(ang)
    t = pos.shape[0]
    pad = HEAD_DIM - ROT_DIM
    c64 = jnp.concatenate([cos, cos, jnp.ones((t, pad), F32)], axis=1)
    sa64 = jnp.concatenate([-sin, jnp.zeros((t, HEAD_DIM - half), F32)], axis=1)
    sb64 = jnp.concatenate([jnp.zeros((t, half), F32), sin, jnp.zeros((t, pad), F32)], axis=1)
    return tuple(jnp.tile(x, (1, 2)) for x in (c64, sa64, sb64))


def kernel(x_prompt, x_sample, cache_k, cache_v, state_conv, page_table, w_in, conv_w, w_mix_out,
           w_kv, w_q, w_o, lambda_q1, lambda_k1, lambda_q2, lambda_k2, subln_g,
           ln1_g, ln1_b, w_gate, w_up, w_down, ln2_g, ln2_b):
    n_p, t_p, _ = x_prompt.shape
    n_s = x_sample.shape[0]
    n_pages = page_table.shape[1]
    page = cache_k.shape[1]
    past_len = n_pages * page

    w_in_b = w_in.astype(BF)
    w_mix_b = w_mix_out.astype(BF)
    w_qkv_b = jnp.concatenate([w_kv, w_q[0]], axis=1).astype(BF)
    w_q_b = w_q.astype(BF)
    w_o_b = w_o.astype(BF)
    wg_b, wu_b, wd_b = w_gate.astype(BF), w_up.astype(BF), w_down.astype(BF)
    lams = (lambda_q1, lambda_k1, lambda_q2, lambda_k2)
    vec = lambda a, l: a[l].reshape(1, D_MODEL)

    def post(x, a, w, l, tm, head_major=False):
        return _post(x, a, w, vec(ln1_g, l), vec(ln1_b, l), wg_b[l], wu_b[l], wd_b[l],
                     vec(ln2_g, l), vec(ln2_b, l), tm, head_major)

    tabs_p = _rope_tables(jnp.arange(t_p))
    tabs_s = tuple(jnp.broadcast_to(x, (n_s, LANES))
                   for x in _rope_tables(past_len + jnp.arange(1)))
    ck = cache_k.reshape(cache_k.shape[0], page, QK_DIM)
    cv = cache_v.reshape(cache_v.shape[0], page, VO_DIM)
    pt_flat = page_table.reshape(-1)

    x = x_prompt
    conv_p = []
    for l in range(N_A_LAYERS):
        a, st = _mixer_prompt(x, w_in_b[l], conv_w[l])
        conv_p.append(st[:, SUBLANES - 2:, :])
        x = post(x.reshape(n_p * t_p, D_MODEL), a.reshape(n_p * t_p, D_MODEL), w_mix_b[l], l,
                 256).reshape(n_p, t_p, D_MODEL)
    x = x.reshape(n_p * t_p, D_MODEL)
    k_p, v_p, kb, vb, q = _qkv(x, w_qkv_b, tabs_p, 256, True, n_p)
    for l in range(N_A_LAYERS, DEPTH):
        i = l - N_A_LAYERS
        lam_init = 0.8 - 0.6 * math.exp(-0.3 * l)
        if i > 0:
            (q,) = _qkv(x, w_q_b[i], tabs_p, 256, False, n_p)
        o = _attn_prompt(q, kb, vb, lams, subln_g, i, lam_init)
        x = post(x, o, w_o_b[i], l, 256, head_major=True)
    y_prompt = x.reshape(n_p, t_p, D_MODEL)

    x = x_sample.reshape(n_s, D_MODEL)
    conv_s = []
    for l in range(N_A_LAYERS):
        a, u = _mixer_dec(x, state_conv[l, :, 0, :], state_conv[l, :, 1, :], w_in_b[l], conv_w[l])
        conv_s.append(jnp.stack([state_conv[l, :, 1, :], u], axis=1))
        x = post(x, a, w_mix_b[l], l, n_s)
    k_s, v_s, kb_s, vb_s, q_s = _qkv(x, w_qkv_b, tabs_s, n_s, True, 0)
    kn = kb_s.reshape(n_s, 1, QK_DIM)
    vn = vb_s.reshape(n_s, 1, VO_DIM)
    for l in range(N_A_LAYERS, DEPTH):
        i = l - N_A_LAYERS
        lam_init = 0.8 - 0.6 * math.exp(-0.3 * l)
        if i > 0:
            (q_s,) = _qkv(x, w_q_b[i], tabs_s, n_s, False, 0)
        o = _attn_dec(pt_flat, q_s.reshape(n_s, 1, QK_DIM), kn, vn, ck, cv, lams, subln_g,
                      i, lam_init, n_pages)
        x = post(x, o.reshape(n_s, VO_DIM), w_o_b[i], l, n_s)

    return (y_prompt,
            x.reshape(n_s, 1, D_MODEL),
            k_p.reshape(n_p, t_p, N_HEADS, 2, HEAD_DIM),
            v_p.reshape(n_p, t_p, N_HEADS, V_DIM),
            jnp.stack(conv_p),
            k_s.reshape(n_s, 1, N_HEADS, 2, HEAD_DIM),
            v_s.reshape(n_s, 1, N_HEADS, V_DIM),
            jnp.stack(conv_s))
```

```python
import functools
import math

import jax
import jax.numpy as jnp
from jax import lax
from jax.experimental import pallas as pl
from jax.experimental.pallas import tpu as pltpu

D_MODEL = 1024
DEPTH = 4
N_A_LAYERS = DEPTH // 2
N_HEADS = 8
HEAD_DIM = 64
V_DIM = 2 * HEAD_DIM
QK_DIM = N_HEADS * 2 * HEAD_DIM
VO_DIM = N_HEADS * V_DIM
ROT_DIM = HEAD_DIM // 4
ROPE_THETA = 500000.0
D_FF = 2816
ALPHA = (2 * DEPTH) ** 0.25
LN_EPS = 1e-5
LANES = 128
SUBLANES = 8
NEG = -1e30
VMEM_LIMIT = 56 * 1024 * 1024

BF = jnp.bfloat16
F32 = jnp.float32


def _dot(a, b):
    return jnp.dot(a, b, preferred_element_type=F32)


def _dot_nt(a, b):
    return lax.dot_general(a, b, (((1,), (1,)), ((), ())), preferred_element_type=F32)


def _ln(x, g, b):
    mu = jnp.mean(x, axis=-1, keepdims=True)
    xc = x - mu
    var = jnp.mean(xc * xc, axis=-1, keepdims=True)
    return xc * lax.rsqrt(var + LN_EPS) * g + b


def _lam(lq1, lk1, lq2, lk2, i, lam_init):
    a = jnp.sum(lq1[i:i + 1, :] * lk1[i:i + 1, :], axis=-1, keepdims=True)
    b = jnp.sum(lq2[i:i + 1, :] * lk2[i:i + 1, :], axis=-1, keepdims=True)
    return jnp.exp(a) - jnp.exp(b) + lam_init


def _const_spec(shape):
    nd = len(shape)
    return pl.BlockSpec(shape, lambda *_: (0,) * nd, pipeline_mode=pl.Buffered(1))


def _params(sem):
    return pltpu.CompilerParams(dimension_semantics=sem, vmem_limit_bytes=VMEM_LIMIT)


def _mixer_kernel(x_ref, win_ref, cw_ref, a_ref, st_ref, carry_ref):
    @pl.when(pl.program_id(1) == 0)
    def _():
        carry_ref[...] = jnp.zeros_like(carry_ref)

    z = _dot(x_ref[0].astype(BF), win_ref[...])
    b = z[:, :D_MODEL]
    u = z[:, D_MODEL:2 * D_MODEL] * z[:, 2 * D_MODEL:]
    tm = u.shape[0]
    prev = carry_ref[...]
    row = lax.broadcasted_iota(jnp.int32, (SUBLANES, D_MODEL), 0)

    def shifted(k):
        us = pltpu.roll(u, k, 0)
        head = jnp.where(row < k, pltpu.roll(prev, k, 0), us[:SUBLANES])
        return jnp.concatenate([head, us[SUBLANES:]], axis=0)

    conv = cw_ref[0:1, :] * shifted(2) + cw_ref[1:2, :] * shifted(1) + cw_ref[2:3, :] * u
    a_ref[0] = (b * conv).astype(BF)
    last = u[tm - SUBLANES:]
    carry_ref[...] = last
    st_ref[0] = last


def _mixer_prompt(x, w_in, cw, tm=512):
    n, t, _ = x.shape
    return pl.pallas_call(
        _mixer_kernel,
        grid=(n, t // tm),
        in_specs=[pl.BlockSpec((1, tm, D_MODEL), lambda i, j: (i, j, 0)),
                  _const_spec((D_MODEL, 3 * D_MODEL)),
                  _const_spec((3, D_MODEL))],
        out_specs=[pl.BlockSpec((1, tm, D_MODEL), lambda i, j: (i, j, 0)),
                   pl.BlockSpec((1, SUBLANES, D_MODEL), lambda i, j: (i, 0, 0))],
        out_shape=[jax.ShapeDtypeStruct((n, t, D_MODEL), BF),
                   jax.ShapeDtypeStruct((n, SUBLANES, D_MODEL), F32)],
        scratch_shapes=[pltpu.VMEM((SUBLANES, D_MODEL), F32)],
        compiler_params=_params(("arbitrary", "arbitrary")),
        name="mixer_prompt",
    )(x, w_in, cw)


def _mixer_dec_kernel(x_ref, s0_ref, s1_ref, win_ref, cw_ref, a_ref, u_ref):
    z = _dot(x_ref[...].astype(BF), win_ref[...])
    b = z[:, :D_MODEL]
    u = z[:, D_MODEL:2 * D_MODEL] * z[:, 2 * D_MODEL:]
    conv = cw_ref[0:1, :] * s0_ref[...] + cw_ref[1:2, :] * s1_ref[...] + cw_ref[2:3, :] * u
    a_ref[...] = (b * conv).astype(BF)
    u_ref[...] = u


def _mixer_dec(x, s0, s1, w_in, cw):
    r = x.shape[0]
    row = pl.BlockSpec((r, D_MODEL), lambda i: (0, 0))
    return pl.pallas_call(
        _mixer_dec_kernel,
        grid=(1,),
        in_specs=[row, row, row, _const_spec((D_MODEL, 3 * D_MODEL)), _const_spec((3, D_MODEL))],
        out_specs=[row, row],
        out_shape=[jax.ShapeDtypeStruct((r, D_MODEL), BF), jax.ShapeDtypeStruct((r, D_MODEL), F32)],
        compiler_params=_params(("arbitrary",)),
        name="mixer_dec",
    )(x, s0, s1, w_in, cw)


def _post_kernel(x_ref, a_ref, w_ref, g1_ref, b1_ref, wg_ref, wu_ref, wd_ref, g2_ref, b2_ref,
                 o_ref):
    x1 = _ln(ALPHA * x_ref[...] + _dot(a_ref[...], w_ref[...]), g1_ref[...], b1_ref[...])
    x1b = x1.astype(BF)
    hh = (jax.nn.silu(_dot(x1b, wg_ref[...])) * _dot(x1b, wu_ref[...])).astype(BF)
    o_ref[...] = _ln(ALPHA * x1 + _dot(hh, wd_ref[...]), g2_ref[...], b2_ref[...])


def _post(x, a, w, g1, b1, wg, wu, wd, g2, b2, tm):
    r = x.shape[0]
    row = pl.BlockSpec((tm, D_MODEL), lambda i: (i, 0))
    vec = _const_spec((1, D_MODEL))
    return pl.pallas_call(
        _post_kernel,
        grid=(r // tm,),
        in_specs=[row, row, _const_spec((D_MODEL, D_MODEL)), vec, vec,
                  _const_spec((D_MODEL, D_FF)), _const_spec((D_MODEL, D_FF)),
                  _const_spec((D_FF, D_MODEL)), vec, vec],
        out_specs=row,
        out_shape=jax.ShapeDtypeStruct((r, D_MODEL), F32),
        compiler_params=_params(("arbitrary",)),
        name="post",
    )(x, a, w, g1, b1, wg, wu, wd, g2, b2)


def _rope(z, c, sa, sb):
    parts = []
    for j in range(z.shape[1] // LANES):
        zj = z[:, j * LANES:(j + 1) * LANES]
        parts.append(zj * c + pltpu.roll(zj, LANES - ROT_DIM // 2, 1) * sa
                     + pltpu.roll(zj, ROT_DIM // 2, 1) * sb)
    return jnp.concatenate(parts, axis=1)


def _rope_t(zt, ct, st):
    half = ROT_DIM // 2
    tiles = []
    for g in range(zt.shape[0] // HEAD_DIM):
        base = g * HEAD_DIM
        x1 = zt[base:base + half]
        x2 = zt[base + half:base + ROT_DIM]
        tiles += [x1 * ct - x2 * st, x2 * ct + x1 * st, zt[base + ROT_DIM:base + HEAD_DIM]]
    return jnp.concatenate(tiles, axis=0)


def _qkv_prompt_kernel(*refs, with_kv):
    if with_kv:
        (x_ref, wk_ref, wv_ref, wvt_ref, wqt_ref, c_ref, sa_ref, sb_ref, ct_ref, st_ref,
         k_ref, v_ref, kb_ref, vt_ref, qt_ref) = refs
    else:
        x_ref, wqt_ref, ct_ref, st_ref, qt_ref = refs
    xb = x_ref[...].astype(BF)
    if with_kv:
        k = _rope(_dot(xb, wk_ref[...]), c_ref[...], sa_ref[...], sb_ref[...])
        k_ref[...] = k
        kb = k.astype(BF)
        v_ref[...] = _dot(xb, wv_ref[...])
        vt = _dot_nt(wvt_ref[...], xb).astype(BF)
        for h in range(N_HEADS):
            kb_ref[0, h] = kb[:, h * V_DIM:(h + 1) * V_DIM]
            vt_ref[0, h] = vt[h * V_DIM:(h + 1) * V_DIM, :]
    qt = _rope_t(_dot_nt(wqt_ref[...], xb), ct_ref[...], st_ref[...]) * (HEAD_DIM ** -0.5)
    qt = qt.astype(BF)
    for h in range(N_HEADS):
        qt_ref[0, h] = qt[h * V_DIM:(h + 1) * V_DIM, :]


def _qkv_prompt(x, ws, tabs, tabs_t, tm, with_kv, n_batch):
    r = x.shape[0]
    t = r // n_batch
    nt = t // tm
    row = pl.BlockSpec((tm, D_MODEL), lambda i: (i, 0))
    wspec = _const_spec((D_MODEL, D_MODEL))
    tab = pl.BlockSpec((tm, LANES), lambda i: (i % nt, 0))
    tab_t = pl.BlockSpec((SUBLANES, tm), lambda i: (0, i % nt))
    nat = pl.BlockSpec((1, N_HEADS, tm, V_DIM), lambda i: (i // nt, 0, i % nt, 0))
    tr = pl.BlockSpec((1, N_HEADS, V_DIM, tm), lambda i: (i // nt, 0, 0, i % nt))
    nat_shape = jax.ShapeDtypeStruct((n_batch, N_HEADS, t, V_DIM), BF)
    tr_shape = jax.ShapeDtypeStruct((n_batch, N_HEADS, V_DIM, t), BF)
    f32_shape = jax.ShapeDtypeStruct((r, D_MODEL), F32)
    if with_kv:
        in_specs = [row, wspec, wspec, wspec, wspec, tab, tab, tab, tab_t, tab_t]
        args = (x, *ws, *tabs, *tabs_t)
        out_specs = [row, row, nat, tr, tr]
        out_shape = [f32_shape, f32_shape, nat_shape, tr_shape, tr_shape]
    else:
        in_specs = [row, wspec, tab_t, tab_t]
        args = (x, *ws, *tabs_t)
        out_specs = [tr]
        out_shape = [tr_shape]
    return pl.pallas_call(
        functools.partial(_qkv_prompt_kernel, with_kv=with_kv),
        grid=(r // tm,),
        in_specs=in_specs, out_specs=out_specs, out_shape=out_shape,
        compiler_params=_params(("arbitrary",)),
        name="qkv_prompt",
    )(*args)


def _qkv_dec_kernel(x_ref, w_ref, c_ref, sa_ref, sb_ref, *out_refs, with_kv):
    z = _dot(x_ref[...].astype(BF), w_ref[...])
    c, sa, sb = c_ref[...], sa_ref[...], sb_ref[...]
    if with_kv:
        k_ref, v_ref, q_ref = out_refs
        k_ref[...] = _rope(z[:, :QK_DIM], c, sa, sb)
        v_ref[...] = z[:, QK_DIM:QK_DIM + VO_DIM]
        zq = z[:, QK_DIM + VO_DIM:]
    else:
        (q_ref,) = out_refs
        zq = z
    q_ref[...] = (_rope(zq, c, sa, sb) * (HEAD_DIM ** -0.5)).astype(BF)


def _qkv_dec(x, w, tabs, with_kv):
    r = x.shape[0]
    row = pl.BlockSpec((r, D_MODEL), lambda i: (0, 0))
    tab = pl.BlockSpec((r, LANES), lambda i: (0, 0))
    f32_shape = jax.ShapeDtypeStruct((r, D_MODEL), F32)
    q_shape = jax.ShapeDtypeStruct((r, D_MODEL), BF)
    return pl.pallas_call(
        functools.partial(_qkv_dec_kernel, with_kv=with_kv),
        grid=(1,),
        in_specs=[row, _const_spec(w.shape), tab, tab, tab],
        out_specs=[row, row, row] if with_kv else [row],
        out_shape=[f32_shape, f32_shape, q_shape] if with_kv else [q_shape],
        compiler_params=_params(("arbitrary",)),
        name="qkv_dec",
    )(x, w, *tabs)


def _attn_kernel(qt_ref, k_ref, vt_ref, lq1_ref, lk1_ref, lq2_ref, lk2_ref, g_ref, o_ref,
                 q2t_scr, m_scr, l_scr, acc_scr, bias_scr, *, layer_i, lam_init, tq):
    b = pl.program_id(0)
    i = pl.program_id(1)
    j = pl.program_id(2)

    @pl.when((b == 0) & (i == 0) & (j == 0))
    def _bias():
        kk = lax.broadcasted_iota(jnp.int32, (tq, 2 * tq), 0)
        qq = lax.broadcasted_iota(jnp.int32, (tq, 2 * tq), 1)
        qq = jnp.where(qq >= tq, qq - tq, qq)
        bias_scr[...] = jnp.where(kk > qq, NEG, 0.0)

    @pl.when(j == 0)
    def _init():
        zero = jnp.zeros((HEAD_DIM, tq), BF)
        for h in range(N_HEADS):
            q2t_scr[h, :HEAD_DIM, :tq] = qt_ref[0, h, :HEAD_DIM, :]
            q2t_scr[h, HEAD_DIM:, :tq] = zero
            q2t_scr[h, :HEAD_DIM, tq:] = zero
            q2t_scr[h, HEAD_DIM:, tq:] = qt_ref[0, h, HEAD_DIM:, :]
        m_scr[...] = jnp.full_like(m_scr, NEG)
        l_scr[...] = jnp.zeros_like(l_scr)
        acc_scr[...] = jnp.zeros_like(acc_scr)

    def step(masked):
        for h in range(N_HEADS):
            s = _dot(k_ref[0, h], q2t_scr[h])
            if masked:
                s = s + bias_scr[...]
            m_prev = m_scr[h]
            m_new = jnp.maximum(m_prev, jnp.max(s, axis=0, keepdims=True))
            alpha = jnp.exp(m_prev - m_new)
            p = jnp.exp(s - m_new)
            l_scr[h] = alpha * l_scr[h] + jnp.sum(p, axis=0, keepdims=True)
            acc_scr[h] = alpha * acc_scr[h] + _dot(vt_ref[0, h], p.astype(BF))
            m_scr[h] = m_new

    @pl.when(j < i)
    def _():
        step(False)

    @pl.when(j == i)
    def _():
        step(True)
        lam = _lam(lq1_ref, lk1_ref, lq2_ref, lk2_ref, layer_i, lam_init)
        g = g_ref[layer_i:layer_i + 1, :] * (1.0 - lam_init)
        for h in range(N_HEADS):
            acc = acc_scr[h]
            inv = 1.0 / l_scr[h]
            ot = acc[:, :tq] * inv[:, :tq] - lam * (acc[:, tq:] * inv[:, tq:])
            ms = jnp.mean(ot * ot, axis=0, keepdims=True)
            o = (ot * lax.rsqrt(ms + LN_EPS)).T * g
            o_ref[0, :, h * V_DIM:(h + 1) * V_DIM] = o.astype(BF)


def _attn_prompt(qt, k, vt, lams, g, layer_i, lam_init, tq=512):
    n, h, t, _ = k.shape
    nq = t // tq
    qspec = pl.BlockSpec((1, h, V_DIM, tq), lambda b, i, j: (b, 0, 0, i))
    kspec = pl.BlockSpec((1, h, tq, V_DIM), lambda b, i, j: (b, 0, jnp.minimum(i, j), 0))
    vspec = pl.BlockSpec((1, h, V_DIM, tq), lambda b, i, j: (b, 0, 0, jnp.minimum(i, j)))
    lspec = _const_spec(lams[0].shape)
    return pl.pallas_call(
        functools.partial(_attn_kernel, layer_i=layer_i, lam_init=lam_init, tq=tq),
        grid=(n, nq, nq),
        in_specs=[qspec, kspec, vspec, lspec, lspec, lspec, lspec, _const_spec(g.shape)],
        out_specs=pl.BlockSpec((1, tq, VO_DIM), lambda b, i, j: (b, i, 0)),
        out_shape=jax.ShapeDtypeStruct((n, t, VO_DIM), BF),
        scratch_shapes=[pltpu.VMEM((h, V_DIM, 2 * tq), BF),
                        pltpu.VMEM((h, 1, 2 * tq), F32),
                        pltpu.VMEM((h, 1, 2 * tq), F32),
                        pltpu.VMEM((h, V_DIM, 2 * tq), F32),
                        pltpu.VMEM((tq, 2 * tq), F32)],
        compiler_params=_params(("arbitrary", "arbitrary", "arbitrary")),
        name="attn_prompt",
    )(qt, k, vt, *lams, g)


def _attn_dec_kernel(pt_ref, q_ref, kn_ref, vn_ref, lq1_ref, lk1_ref, lq2_ref, lk2_ref, g_ref,
                     *rest, layer_i, lam_init, n_pages):
    del pt_ref
    k_refs = rest[:n_pages]
    v_refs = rest[n_pages:2 * n_pages]
    o_ref = rest[2 * n_pages]
    nm = 2 * N_HEADS
    page = k_refs[0].shape[2]
    lam = _lam(lq1_ref, lk1_ref, lq2_ref, lk2_ref, layer_i, lam_init)

    grp = lax.shift_right_logical(lax.broadcasted_iota(jnp.int32, (nm, QK_DIM), 1), 6)
    r = lax.broadcasted_iota(jnp.int32, (nm, QK_DIM), 0)
    sel = r == (grp & 1) * N_HEADS + lax.shift_right_logical(grp, 1)
    qbd = jnp.where(sel, jnp.broadcast_to(q_ref[0].astype(F32), (nm, QK_DIM)), 0.0)
    qbd_b = qbd.astype(BF)

    s = jnp.concatenate([_dot(qbd_b, k_refs[pg][0].astype(BF)) for pg in range(n_pages)],
                        axis=1)
    s_new = jnp.sum(qbd * kn_ref[0].astype(F32), axis=-1, keepdims=True)
    m = jnp.maximum(jnp.max(s, axis=-1, keepdims=True), s_new)
    p = jnp.exp(s - m)
    p_new = jnp.exp(s_new - m)
    l = jnp.sum(p, axis=-1, keepdims=True) + p_new
    pn = p / l
    pn_new = p_new / l
    a = pn[:N_HEADS] - lam * pn[N_HEADS:]
    a_new = pn_new[:N_HEADS] - lam * pn_new[N_HEADS:]
    a2 = jnp.concatenate([a, a], axis=0).astype(BF)

    ast = jnp.concatenate([a2[:, pg * page:(pg + 1) * page] for pg in range(n_pages)], axis=0)
    e_lane = lax.shift_right_logical(
        lax.broadcasted_iota(jnp.int32, (page, page * N_HEADS), 1), 3)
    e_pos = lax.broadcasted_iota(jnp.int32, (page, page * N_HEADS), 0)
    expand = jnp.where(e_lane == e_pos, 1.0, 0.0).astype(BF)
    aexp = _dot(ast, expand)
    keep = ((lax.broadcasted_iota(jnp.int32, (nm, page * N_HEADS), 1) & (N_HEADS - 1))
            == (lax.broadcasted_iota(jnp.int32, (nm, page * N_HEADS), 0) & (N_HEADS - 1)))
    o = jnp.zeros((nm, V_DIM), F32)
    for pg in range(n_pages):
        a_pg = jnp.where(keep, aexp[pg * nm:(pg + 1) * nm], 0.0).astype(BF)
        o = o + _dot(a_pg, v_refs[pg][0].astype(BF))
    o = o[:N_HEADS] + a_new.astype(BF).astype(F32) * vn_ref[0]
    ms = jnp.mean(o * o, axis=-1, keepdims=True)
    g = g_ref[layer_i:layer_i + 1, :] * (1.0 - lam_init)
    o_ref[0] = o * lax.rsqrt(ms + LN_EPS) * g


def _attn_dec(pt_flat, q, kn, vn, ckt, cv, lams, g, layer_i, lam_init, n_pages):
    ns = q.shape[0]
    tok = pl.BlockSpec((1, 1, D_MODEL), lambda s, pt: (s, 0, 0))
    hv = pl.BlockSpec((1, N_HEADS, V_DIM), lambda s, pt: (s, 0, 0))

    def page_spec(arr, pg):
        return pl.BlockSpec((1,) + arr.shape[1:], lambda s, pt: (pt[s * n_pages + pg], 0, 0))

    lspec = _const_spec(lams[0].shape)
    grid_spec = pltpu.PrefetchScalarGridSpec(
        num_scalar_prefetch=1,
        grid=(ns,),
        in_specs=[tok, tok, hv, lspec, lspec, lspec, lspec, _const_spec(g.shape)]
                 + [page_spec(ckt, pg) for pg in range(n_pages)]
                 + [page_spec(cv, pg) for pg in range(n_pages)],
        out_specs=hv,
    )
    return pl.pallas_call(
        functools.partial(_attn_dec_kernel, layer_i=layer_i, lam_init=lam_init, n_pages=n_pages),
        grid_spec=grid_spec,
        out_shape=jax.ShapeDtypeStruct((ns, N_HEADS, V_DIM), F32),
        compiler_params=_params(("arbitrary",)),
        name="attn_dec",
    )(pt_flat, q, kn, vn, *lams, g, *([ckt] * n_pages), *([cv] * n_pages))


def _rope_angles(pos):
    inv = jnp.power(ROPE_THETA, -jnp.arange(0, ROT_DIM, 2, dtype=F32) / ROT_DIM)
    ang = pos.astype(F32)[:, None] * inv[None, :]
    return jnp.cos(ang), jnp.sin(ang)


def _rope_tables(cos, sin):
    half = ROT_DIM // 2
    t = cos.shape[0]
    pad = HEAD_DIM - ROT_DIM
    c64 = jnp.concatenate([cos, cos, jnp.ones((t, pad), F32)], axis=1)
    sa64 = jnp.concatenate([-sin, jnp.zeros((t, HEAD_DIM - half), F32)], axis=1)
    sb64 = jnp.concatenate([jnp.zeros((t, half), F32), sin, jnp.zeros((t, pad), F32)], axis=1)
    return tuple(jnp.tile(x, (1, 2)) for x in (c64, sa64, sb64))


def kernel(x_prompt, x_sample, cache_k, cache_v, state_conv, page_table, w_in, conv_w, w_mix_out,
           w_kv, w_q, w_o, lambda_q1, lambda_k1, lambda_q2, lambda_k2, subln_g,
           ln1_g, ln1_b, w_gate, w_up, w_down, ln2_g, ln2_b):
    n_p, t_p, _ = x_prompt.shape
    n_s = x_sample.shape[0]
    n_pages = page_table.shape[1]
    n_pool, page = cache_k.shape[:2]
    past_len = n_pages * page

    w_in_b = w_in.astype(BF)
    w_mix_b = w_mix_out.astype(BF)
    w_k_b = w_kv[:, :QK_DIM].astype(BF)
    w_v_b = w_kv[:, QK_DIM:].astype(BF)
    w_vt_b = w_v_b.T
    w_qt_b = jnp.swapaxes(w_q, 1, 2).astype(BF)
    w_qkv_b = jnp.concatenate([w_kv, w_q[0]], axis=1).astype(BF)
    w_q_b = w_q.astype(BF)
    w_o_b = w_o.astype(BF)
    wg_b, wu_b, wd_b = w_gate.astype(BF), w_up.astype(BF), w_down.astype(BF)
    lams = (lambda_q1, lambda_k1, lambda_q2, lambda_k2)
    vec = lambda a, l: a[l].reshape(1, D_MODEL)

    def post(x, a, w, l, tm):
        return _post(x, a, w, vec(ln1_g, l), vec(ln1_b, l), wg_b[l], wu_b[l], wd_b[l],
                     vec(ln2_g, l), vec(ln2_b, l), tm)

    cos_p, sin_p = _rope_angles(jnp.arange(t_p))
    tabs_p = _rope_tables(cos_p, sin_p)
    tabs_pt = (cos_p.T, sin_p.T)
    tabs_s = tuple(jnp.broadcast_to(x, (n_s, LANES))
                   for x in _rope_tables(*_rope_angles(past_len + jnp.arange(1))))
    ckt = jnp.transpose(cache_k, (0, 2, 3, 4, 1)).reshape(n_pool, QK_DIM, page)
    cv = cache_v.reshape(n_pool, page * N_HEADS, V_DIM)
    pt_flat = page_table.reshape(-1)

    x = x_prompt
    conv_p = []
    for l in range(N_A_LAYERS):
        a, st = _mixer_prompt(x, w_in_b[l], conv_w[l])
        conv_p.append(st[:, SUBLANES - 2:, :])
        x = post(x.reshape(n_p * t_p, D_MODEL), a.reshape(n_p * t_p, D_MODEL), w_mix_b[l], l,
                 256).reshape(n_p, t_p, D_MODEL)
    x = x.reshape(n_p * t_p, D_MODEL)
    k_p, v_p, kb, vt, qt = _qkv_prompt(x, (w_k_b, w_v_b, w_vt_b, w_qt_b[0]), tabs_p, tabs_pt,
                                        256, True, n_p)
    for l in range(N_A_LAYERS, DEPTH):
        i = l - N_A_LAYERS
        lam_init = 0.8 - 0.6 * math.exp(-0.3 * l)
        if i > 0:
            (qt,) = _qkv_prompt(x, (w_qt_b[i],), tabs_p, tabs_pt, 256, False, n_p)
        o = _attn_prompt(qt, kb, vt, lams, subln_g, i, lam_init)
        x = post(x, o.reshape(n_p * t_p, VO_DIM), w_o_b[i], l, 256)
    y_prompt = x.reshape(n_p, t_p, D_MODEL)

    x = x_sample.reshape(n_s, D_MODEL)
    conv_s = []
    for l in range(N_A_LAYERS):
        a, u = _mixer_dec(x, state_conv[l, :, 0, :], state_conv[l, :, 1, :], w_in_b[l], conv_w[l])
        conv_s.append(jnp.stack([state_conv[l, :, 1, :], u], axis=1))
        x = post(x, a, w_mix_b[l], l, n_s)
    k_s, v_s, q_s = _qkv_dec(x, w_qkv_b, tabs_s, True)
    kn = k_s.astype(BF).reshape(n_s, 1, QK_DIM)
    vn = v_s.astype(BF).astype(F32).reshape(n_s, N_HEADS, V_DIM)
    for l in range(N_A_LAYERS, DEPTH):
        i = l - N_A_LAYERS
        lam_init = 0.8 - 0.6 * math.exp(-0.3 * l)
        if i > 0:
            (q_s,) = _qkv_dec(x, w_q_b[i], tabs_s, False)
        o = _attn_dec(pt_flat, q_s.reshape(n_s, 1, QK_DIM), kn, vn, ckt, cv, lams, subln_g,
                      i, lam_init, n_pages)
        x = post(x, o.reshape(n_s, VO_DIM).astype(BF), w_o_b[i], l, n_s)

    return (y_prompt,
            x.reshape(n_s, 1, D_MODEL),
            k_p.reshape(n_p, t_p, N_HEADS, 2, HEAD_DIM),
            v_p.reshape(n_p, t_p, N_HEADS, V_DIM),
            jnp.stack(conv_p),
            k_s.reshape(n_s, 1, N_HEADS, 2, HEAD_DIM),
            v_s.reshape(n_s, 1, N_HEADS, V_DIM),
            jnp.stack(conv_s))
```

```python
import functools
import math

import jax
import jax.numpy as jnp
from jax import lax
from jax.experimental import pallas as pl
from jax.experimental.pallas import tpu as pltpu

D_MODEL = 1024
DEPTH = 4
N_A_LAYERS = DEPTH // 2
N_HEADS = 8
HEAD_DIM = 64
V_DIM = 2 * HEAD_DIM
QK_DIM = N_HEADS * 2 * HEAD_DIM
VO_DIM = N_HEADS * V_DIM
ROT_DIM = HEAD_DIM // 4
ROPE_THETA = 500000.0
D_FF = 2816
ALPHA = (2 * DEPTH) ** 0.25
LN_EPS = 1e-5
LANES = 128
SUBLANES = 8
NEG = -1e30
Q_SCALE_LOG2 = HEAD_DIM ** -0.5 * math.log2(math.e)
ATTN_CHUNKS = 4
ACC_ROWS = V_DIM + 16
VMEM_LIMIT = 56 * 1024 * 1024

BF = jnp.bfloat16
F32 = jnp.float32


def _dot(a, b):
    return jnp.dot(a, b, preferred_element_type=F32)


def _dot_nt(a, b):
    return lax.dot_general(a, b, (((1,), (1,)), ((), ())), preferred_element_type=F32)


def _ln(x, g, b):
    mu = jnp.mean(x, axis=-1, keepdims=True)
    xc = x - mu
    var = jnp.mean(xc * xc, axis=-1, keepdims=True)
    return xc * lax.rsqrt(var + LN_EPS) * g + b


def _lam(lq1, lk1, lq2, lk2, i, lam_init):
    a = jnp.sum(lq1[i:i + 1, :] * lk1[i:i + 1, :], axis=-1, keepdims=True)
    b = jnp.sum(lq2[i:i + 1, :] * lk2[i:i + 1, :], axis=-1, keepdims=True)
    return jnp.exp(a) - jnp.exp(b) + lam_init


def _const_spec(shape):
    nd = len(shape)
    return pl.BlockSpec(shape, lambda *_: (0,) * nd, pipeline_mode=pl.Buffered(1))


def _params(sem):
    return pltpu.CompilerParams(dimension_semantics=sem, vmem_limit_bytes=VMEM_LIMIT)


def _mixer_kernel(x_ref, win_ref, cw_ref, a_ref, st_ref, carry_ref):
    @pl.when(pl.program_id(1) == 0)
    def _():
        carry_ref[...] = jnp.zeros_like(carry_ref)

    z = _dot(x_ref[0].astype(BF), win_ref[...])
    b = z[:, :D_MODEL]
    u = z[:, D_MODEL:2 * D_MODEL] * z[:, 2 * D_MODEL:]
    tm = u.shape[0]
    prev = carry_ref[...]
    row = lax.broadcasted_iota(jnp.int32, (SUBLANES, D_MODEL), 0)

    def shifted(k):
        us = pltpu.roll(u, k, 0)
        head = jnp.where(row < k, pltpu.roll(prev, k, 0), us[:SUBLANES])
        return jnp.concatenate([head, us[SUBLANES:]], axis=0)

    conv = cw_ref[0:1, :] * shifted(2) + cw_ref[1:2, :] * shifted(1) + cw_ref[2:3, :] * u
    a_ref[0] = (b * conv).astype(BF)
    last = u[tm - SUBLANES:]
    carry_ref[...] = last
    st_ref[0] = last


def _mixer_prompt(x, w_in, cw, tm=512):
    n, t, _ = x.shape
    return pl.pallas_call(
        _mixer_kernel,
        grid=(n, t // tm),
        in_specs=[pl.BlockSpec((1, tm, D_MODEL), lambda i, j: (i, j, 0)),
                  _const_spec((D_MODEL, 3 * D_MODEL)),
                  _const_spec((3, D_MODEL))],
        out_specs=[pl.BlockSpec((1, tm, D_MODEL), lambda i, j: (i, j, 0)),
                   pl.BlockSpec((1, SUBLANES, D_MODEL), lambda i, j: (i, 0, 0))],
        out_shape=[jax.ShapeDtypeStruct((n, t, D_MODEL), BF),
                   jax.ShapeDtypeStruct((n, SUBLANES, D_MODEL), F32)],
        scratch_shapes=[pltpu.VMEM((SUBLANES, D_MODEL), F32)],
        compiler_params=_params(("arbitrary", "arbitrary")),
        name="mixer_prompt",
    )(x, w_in, cw)


def _mixer_dec_kernel(x_ref, s0_ref, s1_ref, win_ref, cw_ref, a_ref, u_ref):
    z = _dot(x_ref[...].astype(BF), win_ref[...])
    b = z[:, :D_MODEL]
    u = z[:, D_MODEL:2 * D_MODEL] * z[:, 2 * D_MODEL:]
    conv = cw_ref[0:1, :] * s0_ref[...] + cw_ref[1:2, :] * s1_ref[...] + cw_ref[2:3, :] * u
    a_ref[...] = (b * conv).astype(BF)
    u_ref[...] = u


def _mixer_dec(x, s0, s1, w_in, cw):
    r = x.shape[0]
    row = pl.BlockSpec((r, D_MODEL), lambda i: (0, 0))
    return pl.pallas_call(
        _mixer_dec_kernel,
        grid=(1,),
        in_specs=[row, row, row, _const_spec((D_MODEL, 3 * D_MODEL)), _const_spec((3, D_MODEL))],
        out_specs=[row, row],
        out_shape=[jax.ShapeDtypeStruct((r, D_MODEL), BF), jax.ShapeDtypeStruct((r, D_MODEL), F32)],
        compiler_params=_params(("arbitrary",)),
        name="mixer_dec",
    )(x, s0, s1, w_in, cw)


def _post_kernel(x_ref, a_ref, w_ref, g1_ref, b1_ref, wg_ref, wu_ref, wd_ref, g2_ref, b2_ref,
                 o_ref):
    x1 = _ln(ALPHA * x_ref[...] + _dot(a_ref[...], w_ref[...]), g1_ref[...], b1_ref[...])
    x1b = x1.astype(BF)
    hh = (jax.nn.silu(_dot(x1b, wg_ref[...])) * _dot(x1b, wu_ref[...])).astype(BF)
    o_ref[...] = _ln(ALPHA * x1 + _dot(hh, wd_ref[...]), g2_ref[...], b2_ref[...])


def _post(x, a, w, g1, b1, wg, wu, wd, g2, b2, tm):
    r = x.shape[0]
    row = pl.BlockSpec((tm, D_MODEL), lambda i: (i, 0))
    vec = _const_spec((1, D_MODEL))
    return pl.pallas_call(
        _post_kernel,
        grid=(r // tm,),
        in_specs=[row, row, _const_spec((D_MODEL, D_MODEL)), vec, vec,
                  _const_spec((D_MODEL, D_FF)), _const_spec((D_MODEL, D_FF)),
                  _const_spec((D_FF, D_MODEL)), vec, vec],
        out_specs=row,
        out_shape=jax.ShapeDtypeStruct((r, D_MODEL), F32),
        compiler_params=_params(("arbitrary",)),
        name="post",
    )(x, a, w, g1, b1, wg, wu, wd, g2, b2)


def _rope(z, c, sa, sb):
    parts = []
    for j in range(z.shape[1] // LANES):
        zj = z[:, j * LANES:(j + 1) * LANES]
        parts.append(zj * c + pltpu.roll(zj, LANES - ROT_DIM // 2, 1) * sa
                     + pltpu.roll(zj, ROT_DIM // 2, 1) * sb)
    return jnp.concatenate(parts, axis=1)


def _rope_t(zt, ct, st):
    half = ROT_DIM // 2
    tiles = []
    for g in range(zt.shape[0] // HEAD_DIM):
        base = g * HEAD_DIM
        x1 = zt[base:base + half]
        x2 = zt[base + half:base + ROT_DIM]
        tiles += [x1 * ct - x2 * st, x2 * ct + x1 * st, zt[base + ROT_DIM:base + HEAD_DIM]]
    return jnp.concatenate(tiles, axis=0)


def _qkv_prompt_kernel(*refs, with_kv):
    if with_kv:
        (x_ref, wk_ref, wkt_ref, wv_ref, wvt_ref, wqt_ref, c_ref, sa_ref, sb_ref, ct_ref, st_ref,
         kt_ref, v_ref, kb_ref, vt_ref, qt_ref) = refs
    else:
        x_ref, wqt_ref, ct_ref, st_ref, qt_ref = refs
    xb = x_ref[...].astype(BF)
    if with_kv:
        kt_ref[0] = _rope_t(_dot_nt(wkt_ref[...], xb), ct_ref[...], st_ref[...])
        kb = _rope(_dot(xb, wk_ref[...]), c_ref[...], sa_ref[...], sb_ref[...]).astype(BF)
        v_ref[...] = _dot(xb, wv_ref[...])
        vt = _dot_nt(wvt_ref[...], xb).astype(BF)
        for h in range(N_HEADS):
            kb_ref[0, h] = kb[:, h * V_DIM:(h + 1) * V_DIM]
            vt_ref[0, h] = vt[h * V_DIM:(h + 1) * V_DIM, :]
    qt = _rope_t(_dot_nt(wqt_ref[...], xb), ct_ref[...], st_ref[...]) * Q_SCALE_LOG2
    qt = qt.astype(BF)
    for h in range(N_HEADS):
        qt_ref[0, h] = qt[h * V_DIM:(h + 1) * V_DIM, :]


def _qkv_prompt(x, ws, tabs, tabs_t, tm, with_kv, n_batch):
    r = x.shape[0]
    t = r // n_batch
    nt = t // tm
    row = pl.BlockSpec((tm, D_MODEL), lambda i: (i, 0))
    wspec = _const_spec((D_MODEL, D_MODEL))
    tab = pl.BlockSpec((tm, LANES), lambda i: (i % nt, 0))
    tab_t = pl.BlockSpec((SUBLANES, tm), lambda i: (0, i % nt))
    nat = pl.BlockSpec((1, N_HEADS, tm, V_DIM), lambda i: (i // nt, 0, i % nt, 0))
    tr = pl.BlockSpec((1, N_HEADS, V_DIM, tm), lambda i: (i // nt, 0, 0, i % nt))
    nat_shape = jax.ShapeDtypeStruct((n_batch, N_HEADS, t, V_DIM), BF)
    tr_shape = jax.ShapeDtypeStruct((n_batch, N_HEADS, V_DIM, t), BF)
    f32_shape = jax.ShapeDtypeStruct((r, D_MODEL), F32)
    if with_kv:
        in_specs = [row, wspec, wspec, wspec, wspec, wspec, tab, tab, tab, tab_t, tab_t]
        args = (x, *ws, *tabs, *tabs_t)
        out_specs = [pl.BlockSpec((1, QK_DIM, tm), lambda i: (i // nt, 0, i % nt)),
                     row, nat, tr, tr]
        out_shape = [jax.ShapeDtypeStruct((n_batch, QK_DIM, t), F32),
                     f32_shape, nat_shape, tr_shape, tr_shape]
    else:
        in_specs = [row, wspec, tab_t, tab_t]
        args = (x, *ws, *tabs_t)
        out_specs = [tr]
        out_shape = [tr_shape]
    return pl.pallas_call(
        functools.partial(_qkv_prompt_kernel, with_kv=with_kv),
        grid=(r // tm,),
        in_specs=in_specs, out_specs=out_specs, out_shape=out_shape,
        compiler_params=_params(("arbitrary",)),
        name="qkv_prompt",
    )(*args)


def _qkv_dec_kernel(x_ref, w_ref, c_ref, sa_ref, sb_ref, *out_refs, with_kv):
    z = _dot(x_ref[...].astype(BF), w_ref[...])
    c, sa, sb = c_ref[...], sa_ref[...], sb_ref[...]
    if with_kv:
        k_ref, v_ref, q_ref = out_refs
        k_ref[...] = _rope(z[:, :QK_DIM], c, sa, sb)
        v_ref[...] = z[:, QK_DIM:QK_DIM + VO_DIM]
        zq = z[:, QK_DIM + VO_DIM:]
    else:
        (q_ref,) = out_refs
        zq = z
    q_ref[...] = (_rope(zq, c, sa, sb) * (HEAD_DIM ** -0.5)).astype(BF)


def _qkv_dec(x, w, tabs, with_kv):
    r = x.shape[0]
    row = pl.BlockSpec((r, D_MODEL), lambda i: (0, 0))
    tab = pl.BlockSpec((r, LANES), lambda i: (0, 0))
    f32_shape = jax.ShapeDtypeStruct((r, D_MODEL), F32)
    q_shape = jax.ShapeDtypeStruct((r, D_MODEL), BF)
    return pl.pallas_call(
        functools.partial(_qkv_dec_kernel, with_kv=with_kv),
        grid=(1,),
        in_specs=[row, _const_spec(w.shape), tab, tab, tab],
        out_specs=[row, row, row] if with_kv else [row],
        out_shape=[f32_shape, f32_shape, q_shape] if with_kv else [q_shape],
        compiler_params=_params(("arbitrary",)),
        name="qkv_dec",
    )(x, w, *tabs)


def _attn_kernel(it_ref, jt_ref, qt_ref, k_ref, vt_ref, lq1_ref, lk1_ref, lq2_ref, lk2_ref, g_ref,
                 o_ref, q2t_scr, m_scr, acc_scr, bias_scr, *, layer_i, lam_init, tq):
    b = pl.program_id(0)
    step_id = pl.program_id(1)
    i = it_ref[step_id]
    j = jt_ref[step_id]
    cw = 2 * tq // ATTN_CHUNKS
    cols = [slice(c * cw, (c + 1) * cw) for c in range(ATTN_CHUNKS)]

    @pl.when((b == 0) & (step_id == 0))
    def _bias():
        kk = lax.broadcasted_iota(jnp.int32, (tq, 2 * tq), 0)
        qq = lax.broadcasted_iota(jnp.int32, (tq, 2 * tq), 1)
        qq = jnp.where(qq >= tq, qq - tq, qq)
        bias_scr[...] = jnp.where(kk > qq, NEG, 0.0)

    @pl.when(j == 0)
    def _init():
        zero = jnp.zeros((HEAD_DIM, tq), BF)
        for h in range(N_HEADS):
            q2t_scr[h, :HEAD_DIM, :tq] = qt_ref[0, h, :HEAD_DIM, :]
            q2t_scr[h, HEAD_DIM:, :tq] = zero
            q2t_scr[h, :HEAD_DIM, tq:] = zero
            q2t_scr[h, HEAD_DIM:, tq:] = qt_ref[0, h, HEAD_DIM:, :]
        m_scr[...] = jnp.full_like(m_scr, NEG)
        acc_scr[...] = jnp.zeros_like(acc_scr)

    def step(masked):
        ones = jnp.ones((ACC_ROWS - V_DIM, tq), BF)

        def scores(h, c):
            s = _dot(k_ref[0, h], q2t_scr[h, :, cols[c]])
            if masked:
                s = s + bias_scr[:, cols[c]]
            return s

        s_next = [scores(0, c) for c in range(ATTN_CHUNKS)]
        for h in range(N_HEADS):
            s_cur, s_next = s_next, []
            va = jnp.concatenate([vt_ref[0, h], ones], axis=0)
            for c in range(ATTN_CHUNKS):
                if h + 1 < N_HEADS:
                    s_next.append(scores(h + 1, c))
                s = s_cur[c]
                m_prev = m_scr[h, :, cols[c]]
                m_new = jnp.maximum(m_prev, jnp.max(s, axis=0, keepdims=True))
                alpha = jnp.exp2(m_prev - m_new)
                p = jnp.exp2(s - m_new).astype(BF)
                acc_scr[h, :, cols[c]] = alpha * acc_scr[h, :, cols[c]] + _dot(va, p)
                m_scr[h, :, cols[c]] = m_new

    @pl.when(j < i)
    def _():
        step(False)

    @pl.when(j == i)
    def _():
        step(True)
        lam = _lam(lq1_ref, lk1_ref, lq2_ref, lk2_ref, layer_i, lam_init)
        g = g_ref[layer_i:layer_i + 1, :] * (1.0 - lam_init)
        for h in range(N_HEADS):
            acc = acc_scr[h, :V_DIM, :]
            inv = 1.0 / acc_scr[h, V_DIM:V_DIM + 1, :]
            ot = acc[:, :tq] * inv[:, :tq] - lam * (acc[:, tq:] * inv[:, tq:])
            ms = jnp.mean(ot * ot, axis=0, keepdims=True)
            o = (ot * lax.rsqrt(ms + LN_EPS)).T * g
            o_ref[0, :, h * V_DIM:(h + 1) * V_DIM] = o.astype(BF)


def _attn_prompt(qt, k, vt, lams, g, layer_i, lam_init, tq=512):
    n, h, t, _ = k.shape
    nq = t // tq
    pairs = [(i, j) for i in range(nq) for j in range(i + 1)]
    it = jnp.asarray([p[0] for p in pairs], jnp.int32)
    jt = jnp.asarray([p[1] for p in pairs], jnp.int32)
    qspec = pl.BlockSpec((1, h, V_DIM, tq), lambda b, s, it, jt: (b, 0, 0, it[s]))
    kspec = pl.BlockSpec((1, h, tq, V_DIM), lambda b, s, it, jt: (b, 0, jt[s], 0))
    vspec = pl.BlockSpec((1, h, V_DIM, tq), lambda b, s, it, jt: (b, 0, 0, jt[s]))
    lspec = _const_spec(lams[0].shape)
    grid_spec = pltpu.PrefetchScalarGridSpec(
        num_scalar_prefetch=2,
        grid=(n, len(pairs)),
        in_specs=[qspec, kspec, vspec, lspec, lspec, lspec, lspec, _const_spec(g.shape)],
        out_specs=pl.BlockSpec((1, tq, VO_DIM), lambda b, s, it, jt: (b, it[s], 0)),
        scratch_shapes=[pltpu.VMEM((h, V_DIM, 2 * tq), BF),
                        pltpu.VMEM((h, 1, 2 * tq), F32),
                        pltpu.VMEM((h, ACC_ROWS, 2 * tq), F32),
                        pltpu.VMEM((tq, 2 * tq), F32)],
    )
    return pl.pallas_call(
        functools.partial(_attn_kernel, layer_i=layer_i, lam_init=lam_init, tq=tq),
        grid_spec=grid_spec,
        out_shape=jax.ShapeDtypeStruct((n, t, VO_DIM), BF),
        compiler_params=_params(("arbitrary", "arbitrary")),
        name="attn_prompt",
    )(it, jt, qt, k, vt, *lams, g)


def _attn_dec_kernel(pt_ref, q_ref, kn_ref, vn_ref, lq1_ref, lk1_ref, lq2_ref, lk2_ref, g_ref,
                     *rest, layer_i, lam_init, n_pages):
    del pt_ref
    k_refs = rest[:n_pages]
    v_refs = rest[n_pages:2 * n_pages]
    o_ref = rest[2 * n_pages]
    nm = 2 * N_HEADS
    page = k_refs[0].shape[2]
    lam = _lam(lq1_ref, lk1_ref, lq2_ref, lk2_ref, layer_i, lam_init)

    grp = lax.shift_right_logical(lax.broadcasted_iota(jnp.int32, (nm, QK_DIM), 1), 6)
    r = lax.broadcasted_iota(jnp.int32, (nm, QK_DIM), 0)
    sel = r == (grp & 1) * N_HEADS + lax.shift_right_logical(grp, 1)
    qbd = jnp.where(sel, jnp.broadcast_to(q_ref[0].astype(F32), (nm, QK_DIM)), 0.0)
    qbd_b = qbd.astype(BF)

    s = jnp.concatenate([_dot(qbd_b, k_refs[pg][0].astype(BF)) for pg in range(n_pages)],
                        axis=1)
    s_new = jnp.sum(qbd * kn_ref[0].astype(F32), axis=-1, keepdims=True)
    m = jnp.maximum(jnp.max(s, axis=-1, keepdims=True), s_new)
    p = jnp.exp(s - m)
    p_new = jnp.exp(s_new - m)
    l = jnp.sum(p, axis=-1, keepdims=True) + p_new
    pn = p / l
    pn_new = p_new / l
    a = pn[:N_HEADS] - lam * pn[N_HEADS:]
    a_new = pn_new[:N_HEADS] - lam * pn_new[N_HEADS:]
    a2 = jnp.concatenate([a, a], axis=0).astype(BF)

    ast = jnp.concatenate([a2[:, pg * page:(pg + 1) * page] for pg in range(n_pages)], axis=0)
    e_lane = lax.shift_right_logical(
        lax.broadcasted_iota(jnp.int32, (page, page * N_HEADS), 1), 3)
    e_pos = lax.broadcasted_iota(jnp.int32, (page, page * N_HEADS), 0)
    expand = jnp.where(e_lane == e_pos, 1.0, 0.0).astype(BF)
    aexp = _dot(ast, expand)
    keep = ((lax.broadcasted_iota(jnp.int32, (nm, page * N_HEADS), 1) & (N_HEADS - 1))
            == (lax.broadcasted_iota(jnp.int32, (nm, page * N_HEADS), 0) & (N_HEADS - 1)))
    o = jnp.zeros((nm, V_DIM), F32)
    for pg in range(n_pages):
        a_pg = jnp.where(keep, aexp[pg * nm:(pg + 1) * nm], 0.0).astype(BF)
        o = o + _dot(a_pg, v_refs[pg][0].astype(BF))
    o = o[:N_HEADS] + a_new.astype(BF).astype(F32) * vn_ref[0]
    ms = jnp.mean(o * o, axis=-1, keepdims=True)
    g = g_ref[layer_i:layer_i + 1, :] * (1.0 - lam_init)
    o_ref[0] = o * lax.rsqrt(ms + LN_EPS) * g


def _attn_dec(pt_flat, q, kn, vn, ckt, cv, lams, g, layer_i, lam_init, n_pages):
    ns = q.shape[0]
    tok = pl.BlockSpec((1, 1, D_MODEL), lambda s, pt: (s, 0, 0))
    hv = pl.BlockSpec((1, N_HEADS, V_DIM), lambda s, pt: (s, 0, 0))

    def page_spec(arr, pg):
        return pl.BlockSpec((1,) + arr.shape[1:], lambda s, pt: (pt[s * n_pages + pg], 0, 0))

    lspec = _const_spec(lams[0].shape)
    grid_spec = pltpu.PrefetchScalarGridSpec(
        num_scalar_prefetch=1,
        grid=(ns,),
        in_specs=[tok, tok, hv, lspec, lspec, lspec, lspec, _const_spec(g.shape)]
                 + [page_spec(ckt, pg) for pg in range(n_pages)]
                 + [page_spec(cv, pg) for pg in range(n_pages)],
        out_specs=hv,
    )
    return pl.pallas_call(
        functools.partial(_attn_dec_kernel, layer_i=layer_i, lam_init=lam_init, n_pages=n_pages),
        grid_spec=grid_spec,
        out_shape=jax.ShapeDtypeStruct((ns, N_HEADS, V_DIM), F32),
        compiler_params=_params(("arbitrary",)),
        name="attn_dec",
    )(pt_flat, q, kn, vn, *lams, g, *([ckt] * n_pages), *([cv] * n_pages))


def _rope_angles(pos):
    inv = jnp.power(ROPE_THETA, -jnp.arange(0, ROT_DIM, 2, dtype=F32) / ROT_DIM)
    ang = pos.astype(F32)[:, None] * inv[None, :]
    return jnp.cos(ang), jnp.sin(ang)


def _rope_tables(cos, sin):
    half = ROT_DIM // 2
    t = cos.shape[0]
    pad = HEAD_DIM - ROT_DIM
    c64 = jnp.concatenate([cos, cos, jnp.ones((t, pad), F32)], axis=1)
    sa64 = jnp.concatenate([-sin, jnp.zeros((t, HEAD_DIM - half), F32)], axis=1)
    sb64 = jnp.concatenate([jnp.zeros((t, half), F32), sin, jnp.zeros((t, pad), F32)], axis=1)
    return tuple(jnp.tile(x, (1, 2)) for x in (c64, sa64, sb64))


def kernel(x_prompt, x_sample, cache_k, cache_v, state_conv, page_table, w_in, conv_w, w_mix_out,
           w_kv, w_q, w_o, lambda_q1, lambda_k1, lambda_q2, lambda_k2, subln_g,
           ln1_g, ln1_b, w_gate, w_up, w_down, ln2_g, ln2_b):
    n_p, t_p, _ = x_prompt.shape
    n_s = x_sample.shape[0]
    n_pages = page_table.shape[1]
    n_pool, page = cache_k.shape[:2]
    past_len = n_pages * page

    w_in_b = w_in.astype(BF)
    w_mix_b = w_mix_out.astype(BF)
    w_k_b = w_kv[:, :QK_DIM].astype(BF)
    w_v_b = w_kv[:, QK_DIM:].astype(BF)
    w_vt_b = w_v_b.T
    w_qt_b = jnp.swapaxes(w_q, 1, 2).astype(BF)
    w_qkv_b = jnp.concatenate([w_kv, w_q[0]], axis=1).astype(BF)
    w_q_b = w_q.astype(BF)
    w_o_b = w_o.astype(BF)
    wg_b, wu_b, wd_b = w_gate.astype(BF), w_up.astype(BF), w_down.astype(BF)
    lams = (lambda_q1, lambda_k1, lambda_q2, lambda_k2)
    vec = lambda a, l: a[l].reshape(1, D_MODEL)

    def post(x, a, w, l, tm):
        return _post(x, a, w, vec(ln1_g, l), vec(ln1_b, l), wg_b[l], wu_b[l], wd_b[l],
                     vec(ln2_g, l), vec(ln2_b, l), tm)

    cos_p, sin_p = _rope_angles(jnp.arange(t_p))
    tabs_p = _rope_tables(cos_p, sin_p)
    tabs_pt = (cos_p.T, sin_p.T)
    tabs_s = tuple(jnp.broadcast_to(x, (n_s, LANES))
                   for x in _rope_tables(*_rope_angles(past_len + jnp.arange(1))))
    ckt = jnp.transpose(cache_k, (0, 2, 3, 4, 1)).reshape(n_pool, QK_DIM, page)
    cv = cache_v.reshape(n_pool, page * N_HEADS, V_DIM)
    pt_flat = page_table.reshape(-1)

    x = x_prompt
    conv_p = []
    for l in range(N_A_LAYERS):
        a, st = _mixer_prompt(x, w_in_b[l], conv_w[l])
        conv_p.append(st[:, SUBLANES - 2:, :])
        x = post(x.reshape(n_p * t_p, D_MODEL), a.reshape(n_p * t_p, D_MODEL), w_mix_b[l], l,
                 256).reshape(n_p, t_p, D_MODEL)
    x = x.reshape(n_p * t_p, D_MODEL)
    kt_p, v_p, kb, vt, qt = _qkv_prompt(x, (w_k_b, w_k_b.T, w_v_b, w_vt_b, w_qt_b[0]), tabs_p,
                                         tabs_pt, 256, True, n_p)
    k_p = jnp.transpose(kt_p.reshape(n_p, N_HEADS, 2, HEAD_DIM, t_p), (0, 4, 1, 2, 3))
    for l in range(N_A_LAYERS, DEPTH):
        i = l - N_A_LAYERS
        lam_init = 0.8 - 0.6 * math.exp(-0.3 * l)
        if i > 0:
            (qt,) = _qkv_prompt(x, (w_qt_b[i],), tabs_p, tabs_pt, 256, False, n_p)
        o = _attn_prompt(qt, kb, vt, lams, subln_g, i, lam_init)
        x = post(x, o.reshape(n_p * t_p, VO_DIM), w_o_b[i], l, 256)
    y_prompt = x.reshape(n_p, t_p, D_MODEL)

    x = x_sample.reshape(n_s, D_MODEL)
    conv_s = []
    for l in range(N_A_LAYERS):
        a, u = _mixer_dec(x, state_conv[l, :, 0, :], state_conv[l, :, 1, :], w_in_b[l], conv_w[l])
        conv_s.append(jnp.stack([state_conv[l, :, 1, :], u], axis=1))
        x = post(x, a, w_mix_b[l], l, n_s)
    k_s, v_s, q_s = _qkv_dec(x, w_qkv_b, tabs_s, True)
    kn = k_s.astype(BF).reshape(n_s, 1, QK_DIM)
    vn = v_s.astype(BF).astype(F32).reshape(n_s, N_HEADS, V_DIM)
    for l in range(N_A_LAYERS, DEPTH):
        i = l - N_A_LAYERS
        lam_init = 0.8 - 0.6 * math.exp(-0.3 * l)
        if i > 0:
            (q_s,) = _qkv_dec(x, w_q_b[i], tabs_s, False)
        o = _attn_dec(pt_flat, q_s.reshape(n_s, 1, QK_DIM), kn, vn, ckt, cv, lams, subln_g,
                      i, lam_init, n_pages)
        x = post(x, o.reshape(n_s, VO_DIM).astype(BF), w_o_b[i], l, n_s)

    return (y_prompt,
            x.reshape(n_s, 1, D_MODEL),
            k_p,
            v_p.reshape(n_p, t_p, N_HEADS, V_DIM),
            jnp.stack(conv_p),
            k_s.reshape(n_s, 1, N_HEADS, 2, HEAD_DIM),
            v_s.reshape(n_s, 1, N_HEADS, V_DIM),
            jnp.stack(conv_s))
```

```python
import functools
import math

import jax
import jax.numpy as jnp
from jax import lax
from jax.experimental import pallas as pl
from jax.experimental.pallas import tpu as pltpu

D_MODEL = 1024
DEPTH = 4
N_A_LAYERS = DEPTH // 2
N_HEADS = 8
HEAD_DIM = 64
V_DIM = 2 * HEAD_DIM
QK_DIM = N_HEADS * 2 * HEAD_DIM
VO_DIM = N_HEADS * V_DIM
ROT_DIM = HEAD_DIM // 4
ROPE_THETA = 500000.0
D_FF = 2816
ALPHA = (2 * DEPTH) ** 0.25
LN_EPS = 1e-5
LANES = 128
SUBLANES = 8
NEG = -1e30
Q_SCALE_LOG2 = HEAD_DIM ** -0.5 * math.log2(math.e)
ATTN_CHUNKS = 4
ACC_ROWS = V_DIM + 16
VMEM_LIMIT = 56 * 1024 * 1024

BF = jnp.bfloat16
F32 = jnp.float32


def _dot(a, b):
    return jnp.dot(a, b, preferred_element_type=F32)


def _dot_nt(a, b):
    return lax.dot_general(a, b, (((1,), (1,)), ((), ())), preferred_element_type=F32)


def _ln(x, g, b):
    mu = jnp.mean(x, axis=-1, keepdims=True)
    xc = x - mu
    var = jnp.mean(xc * xc, axis=-1, keepdims=True)
    return xc * lax.rsqrt(var + LN_EPS) * g + b


def _lam(lq1, lk1, lq2, lk2, i, lam_init):
    a = jnp.sum(lq1[i:i + 1, :] * lk1[i:i + 1, :], axis=-1, keepdims=True)
    b = jnp.sum(lq2[i:i + 1, :] * lk2[i:i + 1, :], axis=-1, keepdims=True)
    return jnp.exp(a) - jnp.exp(b) + lam_init


def _const_spec(shape):
    nd = len(shape)
    return pl.BlockSpec(shape, lambda *_: (0,) * nd, pipeline_mode=pl.Buffered(1))


def _params(sem):
    return pltpu.CompilerParams(dimension_semantics=sem, vmem_limit_bytes=VMEM_LIMIT)


def _mixer_kernel(x_ref, win_ref, cw_ref, a_ref, st_ref, carry_ref):
    @pl.when(pl.program_id(1) == 0)
    def _():
        carry_ref[...] = jnp.zeros_like(carry_ref)

    z = _dot(x_ref[0].astype(BF), win_ref[...])
    b = z[:, :D_MODEL]
    u = z[:, D_MODEL:2 * D_MODEL] * z[:, 2 * D_MODEL:]
    tm = u.shape[0]
    prev = carry_ref[...]
    row = lax.broadcasted_iota(jnp.int32, (SUBLANES, D_MODEL), 0)

    def shifted(k):
        us = pltpu.roll(u, k, 0)
        head = jnp.where(row < k, pltpu.roll(prev, k, 0), us[:SUBLANES])
        return jnp.concatenate([head, us[SUBLANES:]], axis=0)

    conv = cw_ref[0:1, :] * shifted(2) + cw_ref[1:2, :] * shifted(1) + cw_ref[2:3, :] * u
    a_ref[0] = (b * conv).astype(BF)
    last = u[tm - SUBLANES:]
    carry_ref[...] = last
    st_ref[0] = last


def _mixer_prompt(x, w_in, cw, tm=512):
    n, t, _ = x.shape
    return pl.pallas_call(
        _mixer_kernel,
        grid=(n, t // tm),
        in_specs=[pl.BlockSpec((1, tm, D_MODEL), lambda i, j: (i, j, 0)),
                  _const_spec((D_MODEL, 3 * D_MODEL)),
                  _const_spec((3, D_MODEL))],
        out_specs=[pl.BlockSpec((1, tm, D_MODEL), lambda i, j: (i, j, 0)),
                   pl.BlockSpec((1, SUBLANES, D_MODEL), lambda i, j: (i, 0, 0))],
        out_shape=[jax.ShapeDtypeStruct((n, t, D_MODEL), BF),
                   jax.ShapeDtypeStruct((n, SUBLANES, D_MODEL), F32)],
        scratch_shapes=[pltpu.VMEM((SUBLANES, D_MODEL), F32)],
        compiler_params=_params(("arbitrary", "arbitrary")),
        name="mixer_prompt",
    )(x, w_in, cw)


def _mixer_dec_kernel(x_ref, s0_ref, s1_ref, win_ref, cw_ref, a_ref, u_ref):
    z = _dot(x_ref[...].astype(BF), win_ref[...])
    b = z[:, :D_MODEL]
    u = z[:, D_MODEL:2 * D_MODEL] * z[:, 2 * D_MODEL:]
    conv = cw_ref[0:1, :] * s0_ref[...] + cw_ref[1:2, :] * s1_ref[...] + cw_ref[2:3, :] * u
    a_ref[...] = (b * conv).astype(BF)
    u_ref[...] = u


def _mixer_dec(x, s0, s1, w_in, cw):
    r = x.shape[0]
    row = pl.BlockSpec((r, D_MODEL), lambda i: (0, 0))
    return pl.pallas_call(
        _mixer_dec_kernel,
        grid=(1,),
        in_specs=[row, row, row, _const_spec((D_MODEL, 3 * D_MODEL)), _const_spec((3, D_MODEL))],
        out_specs=[row, row],
        out_shape=[jax.ShapeDtypeStruct((r, D_MODEL), BF), jax.ShapeDtypeStruct((r, D_MODEL), F32)],
        compiler_params=_params(("arbitrary",)),
        name="mixer_dec",
    )(x, s0, s1, w_in, cw)


def _post_kernel(x_ref, a_ref, w_ref, g1_ref, b1_ref, wg_ref, wu_ref, wd_ref, g2_ref, b2_ref,
                 o_ref):
    x1 = _ln(ALPHA * x_ref[...] + _dot(a_ref[...], w_ref[...]), g1_ref[...], b1_ref[...])
    x1b = x1.astype(BF)
    hh = (jax.nn.silu(_dot(x1b, wg_ref[...])) * _dot(x1b, wu_ref[...])).astype(BF)
    o_ref[...] = _ln(ALPHA * x1 + _dot(hh, wd_ref[...]), g2_ref[...], b2_ref[...])


def _post(x, a, w, g1, b1, wg, wu, wd, g2, b2, tm):
    r = x.shape[0]
    row = pl.BlockSpec((tm, D_MODEL), lambda i: (i, 0))
    vec = _const_spec((1, D_MODEL))
    return pl.pallas_call(
        _post_kernel,
        grid=(r // tm,),
        in_specs=[row, row, _const_spec((D_MODEL, D_MODEL)), vec, vec,
                  _const_spec((D_MODEL, D_FF)), _const_spec((D_MODEL, D_FF)),
                  _const_spec((D_FF, D_MODEL)), vec, vec],
        out_specs=row,
        out_shape=jax.ShapeDtypeStruct((r, D_MODEL), F32),
        compiler_params=_params(("arbitrary",)),
        name="post",
    )(x, a, w, g1, b1, wg, wu, wd, g2, b2)


def _rope(z, c, sa, sb):
    parts = []
    for j in range(z.shape[1] // LANES):
        zj = z[:, j * LANES:(j + 1) * LANES]
        parts.append(zj * c + pltpu.roll(zj, LANES - ROT_DIM // 2, 1) * sa
                     + pltpu.roll(zj, ROT_DIM // 2, 1) * sb)
    return jnp.concatenate(parts, axis=1)


def _rope_t(zt, ct, st):
    half = ROT_DIM // 2
    tiles = []
    for g in range(zt.shape[0] // HEAD_DIM):
        base = g * HEAD_DIM
        x1 = zt[base:base + half]
        x2 = zt[base + half:base + ROT_DIM]
        tiles += [x1 * ct - x2 * st, x2 * ct + x1 * st, zt[base + ROT_DIM:base + HEAD_DIM]]
    return jnp.concatenate(tiles, axis=0)


def _qkv_prompt_kernel(*refs, with_kv):
    if with_kv:
        (x_ref, wk_ref, wkt_ref, wv_ref, wvt_ref, wqt_ref, c_ref, sa_ref, sb_ref, ct_ref, st_ref,
         kt_ref, v_ref, kb_ref, vt_ref, qt_ref) = refs
    else:
        x_ref, wqt_ref, ct_ref, st_ref, qt_ref = refs
    xb = x_ref[...].astype(BF)
    if with_kv:
        kt_ref[0] = _rope_t(_dot_nt(wkt_ref[...], xb), ct_ref[...], st_ref[...])
        kb = _rope(_dot(xb, wk_ref[...]), c_ref[...], sa_ref[...], sb_ref[...]).astype(BF)
        v_ref[...] = _dot(xb, wv_ref[...])
        vt = _dot_nt(wvt_ref[...], xb).astype(BF)
        for h in range(N_HEADS):
            kb_ref[0, h] = kb[:, h * V_DIM:(h + 1) * V_DIM]
            vt_ref[0, h] = vt[h * V_DIM:(h + 1) * V_DIM, :]
    qt = _rope_t(_dot_nt(wqt_ref[...], xb), ct_ref[...], st_ref[...]) * Q_SCALE_LOG2
    qt = qt.astype(BF)
    for h in range(N_HEADS):
        qt_ref[0, h] = qt[h * V_DIM:(h + 1) * V_DIM, :]


def _qkv_prompt(x, ws, tabs, tabs_t, tm, with_kv, n_batch):
    r = x.shape[0]
    t = r // n_batch
    nt = t // tm
    row = pl.BlockSpec((tm, D_MODEL), lambda i: (i, 0))
    wspec = _const_spec((D_MODEL, D_MODEL))
    tab = pl.BlockSpec((tm, LANES), lambda i: (i % nt, 0))
    tab_t = pl.BlockSpec((SUBLANES, tm), lambda i: (0, i % nt))
    nat = pl.BlockSpec((1, N_HEADS, tm, V_DIM), lambda i: (i // nt, 0, i % nt, 0))
    tr = pl.BlockSpec((1, N_HEADS, V_DIM, tm), lambda i: (i // nt, 0, 0, i % nt))
    nat_shape = jax.ShapeDtypeStruct((n_batch, N_HEADS, t, V_DIM), BF)
    tr_shape = jax.ShapeDtypeStruct((n_batch, N_HEADS, V_DIM, t), BF)
    f32_shape = jax.ShapeDtypeStruct((r, D_MODEL), F32)
    if with_kv:
        in_specs = [row, wspec, wspec, wspec, wspec, wspec, tab, tab, tab, tab_t, tab_t]
        args = (x, *ws, *tabs, *tabs_t)
        out_specs = [pl.BlockSpec((1, QK_DIM, tm), lambda i: (i // nt, 0, i % nt)),
                     row, nat, tr, tr]
        out_shape = [jax.ShapeDtypeStruct((n_batch, QK_DIM, t), F32),
                     f32_shape, nat_shape, tr_shape, tr_shape]
    else:
        in_specs = [row, wspec, tab_t, tab_t]
        args = (x, *ws, *tabs_t)
        out_specs = [tr]
        out_shape = [tr_shape]
    return pl.pallas_call(
        functools.partial(_qkv_prompt_kernel, with_kv=with_kv),
        grid=(r // tm,),
        in_specs=in_specs, out_specs=out_specs, out_shape=out_shape,
        compiler_params=_params(("arbitrary",)),
        name="qkv_prompt",
    )(*args)


def _qkv_dec_kernel(x_ref, w_ref, c_ref, sa_ref, sb_ref, *out_refs, with_kv):
    z = _dot(x_ref[...].astype(BF), w_ref[...])
    c, sa, sb = c_ref[...], sa_ref[...], sb_ref[...]
    if with_kv:
        k_ref, v_ref, q_ref = out_refs
        k_ref[...] = _rope(z[:, :QK_DIM], c, sa, sb)
        v_ref[...] = z[:, QK_DIM:QK_DIM + VO_DIM]
        zq = z[:, QK_DIM + VO_DIM:]
    else:
        (q_ref,) = out_refs
        zq = z
    q_ref[...] = (_rope(zq, c, sa, sb) * (HEAD_DIM ** -0.5)).astype(BF)


def _qkv_dec(x, w, tabs, with_kv):
    r = x.shape[0]
    row = pl.BlockSpec((r, D_MODEL), lambda i: (0, 0))
    tab = pl.BlockSpec((r, LANES), lambda i: (0, 0))
    f32_shape = jax.ShapeDtypeStruct((r, D_MODEL), F32)
    q_shape = jax.ShapeDtypeStruct((r, D_MODEL), BF)
    return pl.pallas_call(
        functools.partial(_qkv_dec_kernel, with_kv=with_kv),
        grid=(1,),
        in_specs=[row, _const_spec(w.shape), tab, tab, tab],
        out_specs=[row, row, row] if with_kv else [row],
        out_shape=[f32_shape, f32_shape, q_shape] if with_kv else [q_shape],
        compiler_params=_params(("arbitrary",)),
        name="qkv_dec",
    )(x, w, *tabs)


def _decode_half(half, qs_ref, kn_ref, vn_ref, k_refs, v_refs, od_ref, dm_scr, dl_scr, dacc_scr,
                 lam, g):
    nm = 2 * N_HEADS
    hp = len(k_refs)
    page = k_refs[0].shape[2]

    @pl.when(half == 0)
    def _():
        dm_scr[...] = jnp.full_like(dm_scr, NEG)
        dl_scr[...] = jnp.zeros_like(dl_scr)
        dacc_scr[...] = jnp.zeros_like(dacc_scr)

    grp = lax.shift_right_logical(lax.broadcasted_iota(jnp.int32, (nm, QK_DIM), 1), 6)
    r = lax.broadcasted_iota(jnp.int32, (nm, QK_DIM), 0)
    sel = r == (grp & 1) * N_HEADS + lax.shift_right_logical(grp, 1)
    qbd = jnp.where(sel, jnp.broadcast_to(qs_ref[0].astype(F32), (nm, QK_DIM)), 0.0)
    qbd_b = qbd.astype(BF)

    s = jnp.concatenate([_dot(qbd_b, k_refs[pg][0].astype(BF)) for pg in range(hp)], axis=1)
    m_prev = dm_scr[:, :1]
    m_new = jnp.maximum(m_prev, jnp.max(s, axis=-1, keepdims=True))
    alpha = jnp.exp(m_prev - m_new)
    p = jnp.exp(s - m_new)
    l_new = alpha * dl_scr[:, :1] + jnp.sum(p, axis=-1, keepdims=True)
    pb = p.astype(BF)

    pst = jnp.concatenate([pb[:, pg * page:(pg + 1) * page] for pg in range(hp)], axis=0)
    e_lane = lax.shift_right_logical(
        lax.broadcasted_iota(jnp.int32, (page, page * N_HEADS), 1), 3)
    e_pos = lax.broadcasted_iota(jnp.int32, (page, page * N_HEADS), 0)
    expand = jnp.where(e_lane == e_pos, 1.0, 0.0).astype(BF)
    pexp = _dot(pst, expand)
    keep = ((lax.broadcasted_iota(jnp.int32, (nm, page * N_HEADS), 1) & (N_HEADS - 1))
            == (lax.broadcasted_iota(jnp.int32, (nm, page * N_HEADS), 0) & (N_HEADS - 1)))
    pv = jnp.zeros((nm, V_DIM), F32)
    for pg in range(hp):
        p_pg = jnp.where(keep, pexp[pg * nm:(pg + 1) * nm], 0.0).astype(BF)
        pv = pv + _dot(p_pg, v_refs[pg][0].astype(BF))
    acc_new = alpha * dacc_scr[...] + pv
    dm_scr[...] = jnp.broadcast_to(m_new, dm_scr.shape)
    dl_scr[...] = jnp.broadcast_to(l_new, dl_scr.shape)
    dacc_scr[...] = acc_new

    @pl.when(half == 1)
    def _():
        s_tok = jnp.sum(qbd * kn_ref[0].astype(F32), axis=-1, keepdims=True)
        m_fin = jnp.maximum(m_new, s_tok)
        beta = jnp.exp(m_new - m_fin)
        p_tok = jnp.exp(s_tok - m_fin)
        l_fin = beta * l_new + p_tok
        vn2 = jnp.concatenate([vn_ref[0], vn_ref[0]], axis=0)
        on = (beta * acc_new + p_tok.astype(BF).astype(F32) * vn2) / l_fin
        o = on[:N_HEADS] - lam * on[N_HEADS:]
        ms = jnp.mean(o * o, axis=-1, keepdims=True)
        od_ref[0] = o * lax.rsqrt(ms + LN_EPS) * g


def _attn_kernel(it_ref, jt_ref, pt_ref, qt_ref, k_ref, vt_ref, lq1_ref, lk1_ref, lq2_ref,
                 lk2_ref, g_ref, qs_ref, kn_ref, vn_ref, *rest, layer_i, lam_init, tq, hp, n_units):
    del pt_ref
    kp_refs = rest[:hp]
    vp_refs = rest[hp:2 * hp]
    (o_ref, od_ref, q2t_scr, m_scr, acc_scr, bias_scr, dm_scr, dl_scr, dacc_scr) = rest[2 * hp:]
    b = pl.program_id(0)
    step_id = pl.program_id(1)
    i = it_ref[step_id]
    j = jt_ref[step_id]
    cw = 2 * tq // ATTN_CHUNKS
    cols = [slice(c * cw, (c + 1) * cw) for c in range(ATTN_CHUNKS)]
    lam = _lam(lq1_ref, lk1_ref, lq2_ref, lk2_ref, layer_i, lam_init)
    g = g_ref[layer_i:layer_i + 1, :] * (1.0 - lam_init)

    unit = b * pl.num_programs(1) + step_id

    @pl.when(unit < n_units)
    def _decode():
        _decode_half(unit & 1, qs_ref, kn_ref, vn_ref, kp_refs, vp_refs, od_ref,
                     dm_scr, dl_scr, dacc_scr, lam, g)

    @pl.when((b == 0) & (step_id == 0))
    def _bias():
        kk = lax.broadcasted_iota(jnp.int32, (tq, 2 * tq), 0)
        qq = lax.broadcasted_iota(jnp.int32, (tq, 2 * tq), 1)
        qq = jnp.where(qq >= tq, qq - tq, qq)
        bias_scr[...] = jnp.where(kk > qq, NEG, 0.0)

    @pl.when(j == 0)
    def _init():
        zero = jnp.zeros((HEAD_DIM, tq), BF)
        for h in range(N_HEADS):
            q2t_scr[h, :HEAD_DIM, :tq] = qt_ref[0, h, :HEAD_DIM, :]
            q2t_scr[h, HEAD_DIM:, :tq] = zero
            q2t_scr[h, :HEAD_DIM, tq:] = zero
            q2t_scr[h, HEAD_DIM:, tq:] = qt_ref[0, h, HEAD_DIM:, :]
        m_scr[...] = jnp.full_like(m_scr, NEG)
        acc_scr[...] = jnp.zeros_like(acc_scr)

    def step(masked):
        ones = jnp.ones((ACC_ROWS - V_DIM, tq), BF)

        def scores(h, c):
            s = _dot(k_ref[0, h], q2t_scr[h, :, cols[c]])
            if masked:
                s = s + bias_scr[:, cols[c]]
            return s

        s_next = [scores(0, c) for c in range(ATTN_CHUNKS)]
        for h in range(N_HEADS):
            s_cur, s_next = s_next, []
            va = jnp.concatenate([vt_ref[0, h], ones], axis=0)
            for c in range(ATTN_CHUNKS):
                if h + 1 < N_HEADS:
                    s_next.append(scores(h + 1, c))
                s = s_cur[c]
                m_prev = m_scr[h, :, cols[c]]
                m_new = jnp.maximum(m_prev, jnp.max(s, axis=0, keepdims=True))
                alpha = jnp.exp2(m_prev - m_new)
                p = jnp.exp2(s - m_new).astype(BF)
                acc_scr[h, :, cols[c]] = alpha * acc_scr[h, :, cols[c]] + _dot(va, p)
                m_scr[h, :, cols[c]] = m_new

    @pl.when(j < i)
    def _():
        step(False)

    @pl.when(j == i)
    def _():
        step(True)
        for h in range(N_HEADS):
            acc = acc_scr[h, :V_DIM, :]
            inv = 1.0 / acc_scr[h, V_DIM:V_DIM + 1, :]
            ot = acc[:, :tq] * inv[:, :tq] - lam * (acc[:, tq:] * inv[:, tq:])
            ms = jnp.mean(ot * ot, axis=0, keepdims=True)
            o = (ot * lax.rsqrt(ms + LN_EPS)).T * g
            o_ref[0, :, h * V_DIM:(h + 1) * V_DIM] = o.astype(BF)


def _attention(qt, k, vt, pt_flat, qs, kn, vn, ckt, cv, lams, g, layer_i, lam_init, n_pages,
               tq=512):
    n, h, t, _ = k.shape
    ns = qs.shape[0]
    nq = t // tq
    pairs = [(i, j) for i in range(nq) for j in range(i + 1)]
    n_steps = len(pairs)
    it = jnp.asarray([p[0] for p in pairs], jnp.int32)
    jt = jnp.asarray([p[1] for p in pairs], jnp.int32)
    hp = n_pages // 2
    n_units = 2 * ns
    assert n_pages % 2 == 0 and n_units <= n * n_steps

    def seq_of(b, s):
        return jnp.minimum((b * n_steps + s) // 2, ns - 1)

    def page_spec(arr, pg):
        def index(b, s, it, jt, pt):
            return (pt[seq_of(b, s) * n_pages + ((b * n_steps + s) % 2) * hp + pg], 0, 0)
        return pl.BlockSpec((1,) + arr.shape[1:], index)

    qspec = pl.BlockSpec((1, h, V_DIM, tq), lambda b, s, it, jt, pt: (b, 0, 0, it[s]))
    kspec = pl.BlockSpec((1, h, tq, V_DIM), lambda b, s, it, jt, pt: (b, 0, jt[s], 0))
    vspec = pl.BlockSpec((1, h, V_DIM, tq), lambda b, s, it, jt, pt: (b, 0, 0, jt[s]))
    tok = pl.BlockSpec((1, 1, D_MODEL), lambda b, s, it, jt, pt: (seq_of(b, s), 0, 0))
    hv = pl.BlockSpec((1, N_HEADS, V_DIM), lambda b, s, it, jt, pt: (seq_of(b, s), 0, 0))
    lspec = _const_spec(lams[0].shape)
    grid_spec = pltpu.PrefetchScalarGridSpec(
        num_scalar_prefetch=3,
        grid=(n, n_steps),
        in_specs=[qspec, kspec, vspec, lspec, lspec, lspec, lspec, _const_spec(g.shape),
                  tok, tok, hv]
                 + [page_spec(ckt, pg) for pg in range(hp)]
                 + [page_spec(cv, pg) for pg in range(hp)],
        out_specs=[pl.BlockSpec((1, tq, VO_DIM), lambda b, s, it, jt, pt: (b, it[s], 0)), hv],
        scratch_shapes=[pltpu.VMEM((h, V_DIM, 2 * tq), BF),
                        pltpu.VMEM((h, 1, 2 * tq), F32),
                        pltpu.VMEM((h, ACC_ROWS, 2 * tq), F32),
                        pltpu.VMEM((tq, 2 * tq), F32),
                        pltpu.VMEM((2 * N_HEADS, LANES), F32),
                        pltpu.VMEM((2 * N_HEADS, LANES), F32),
                        pltpu.VMEM((2 * N_HEADS, V_DIM), F32)],
    )
    return pl.pallas_call(
        functools.partial(_attn_kernel, layer_i=layer_i, lam_init=lam_init, tq=tq, hp=hp,
                          n_units=n_units),
        grid_spec=grid_spec,
        out_shape=[jax.ShapeDtypeStruct((n, t, VO_DIM), BF),
                   jax.ShapeDtypeStruct((ns, N_HEADS, V_DIM), F32)],
        compiler_params=_params(("arbitrary", "arbitrary")),
        name="attention",
    )(it, jt, pt_flat, qt, k, vt, *lams, g, qs, kn, vn, *([ckt] * hp), *([cv] * hp))


def _rope_angles(pos):
    inv = jnp.power(ROPE_THETA, -jnp.arange(0, ROT_DIM, 2, dtype=F32) / ROT_DIM)
    ang = pos.astype(F32)[:, None] * inv[None, :]
    return jnp.cos(ang), jnp.sin(ang)


def _rope_tables(cos, sin):
    half = ROT_DIM // 2
    t = cos.shape[0]
    pad = HEAD_DIM - ROT_DIM
    c64 = jnp.concatenate([cos, cos, jnp.ones((t, pad), F32)], axis=1)
    sa64 = jnp.concatenate([-sin, jnp.zeros((t, HEAD_DIM - half), F32)], axis=1)
    sb64 = jnp.concatenate([jnp.zeros((t, half), F32), sin, jnp.zeros((t, pad), F32)], axis=1)
    return tuple(jnp.tile(x, (1, 2)) for x in (c64, sa64, sb64))


def kernel(x_prompt, x_sample, cache_k, cache_v, state_conv, page_table, w_in, conv_w, w_mix_out,
           w_kv, w_q, w_o, lambda_q1, lambda_k1, lambda_q2, lambda_k2, subln_g,
           ln1_g, ln1_b, w_gate, w_up, w_down, ln2_g, ln2_b):
    n_p, t_p, _ = x_prompt.shape
    n_s = x_sample.shape[0]
    n_pages = page_table.shape[1]
    n_pool, page = cache_k.shape[:2]
    past_len = n_pages * page

    w_in_b = w_in.astype(BF)
    w_mix_b = w_mix_out.astype(BF)
    w_k_b = w_kv[:, :QK_DIM].astype(BF)
    w_v_b = w_kv[:, QK_DIM:].astype(BF)
    w_vt_b = w_v_b.T
    w_qt_b = jnp.swapaxes(w_q, 1, 2).astype(BF)
    w_qkv_b = jnp.concatenate([w_kv, w_q[0]], axis=1).astype(BF)
    w_q_b = w_q.astype(BF)
    w_o_b = w_o.astype(BF)
    wg_b, wu_b, wd_b = w_gate.astype(BF), w_up.astype(BF), w_down.astype(BF)
    lams = (lambda_q1, lambda_k1, lambda_q2, lambda_k2)
    vec = lambda a, l: a[l].reshape(1, D_MODEL)

    def post(x, a, w, l, tm):
        return _post(x, a, w, vec(ln1_g, l), vec(ln1_b, l), wg_b[l], wu_b[l], wd_b[l],
                     vec(ln2_g, l), vec(ln2_b, l), tm)

    cos_p, sin_p = _rope_angles(jnp.arange(t_p))
    tabs_p = _rope_tables(cos_p, sin_p)
    tabs_pt = (cos_p.T, sin_p.T)
    tabs_s = tuple(jnp.broadcast_to(x, (n_s, LANES))
                   for x in _rope_tables(*_rope_angles(past_len + jnp.arange(1))))
    ckt = jnp.transpose(cache_k, (0, 2, 3, 4, 1)).reshape(n_pool, QK_DIM, page)
    cv = cache_v.reshape(n_pool, page * N_HEADS, V_DIM)
    pt_flat = page_table.reshape(-1)

    x = x_prompt
    xs = x_sample.reshape(n_s, D_MODEL)
    conv_p, conv_s = [], []
    for l in range(N_A_LAYERS):
        a, st = _mixer_prompt(x, w_in_b[l], conv_w[l])
        conv_p.append(st[:, SUBLANES - 2:, :])
        x = post(x.reshape(n_p * t_p, D_MODEL), a.reshape(n_p * t_p, D_MODEL), w_mix_b[l], l,
                 256).reshape(n_p, t_p, D_MODEL)
        a, u = _mixer_dec(xs, state_conv[l, :, 0, :], state_conv[l, :, 1, :], w_in_b[l],
                          conv_w[l])
        conv_s.append(jnp.stack([state_conv[l, :, 1, :], u], axis=1))
        xs = post(xs, a, w_mix_b[l], l, n_s)

    x = x.reshape(n_p * t_p, D_MODEL)
    kt_p, v_p, kb, vt, qt = _qkv_prompt(x, (w_k_b, w_k_b.T, w_v_b, w_vt_b, w_qt_b[0]), tabs_p,
                                         tabs_pt, 256, True, n_p)
    k_p = jnp.transpose(kt_p.reshape(n_p, N_HEADS, 2, HEAD_DIM, t_p), (0, 4, 1, 2, 3))
    k_s, v_s, q_s = _qkv_dec(xs, w_qkv_b, tabs_s, True)
    kn = k_s.astype(BF).reshape(n_s, 1, QK_DIM)
    vn = v_s.astype(BF).astype(F32).reshape(n_s, N_HEADS, V_DIM)

    for l in range(N_A_LAYERS, DEPTH):
        i = l - N_A_LAYERS
        lam_init = 0.8 - 0.6 * math.exp(-0.3 * l)
        if i > 0:
            (qt,) = _qkv_prompt(x, (w_qt_b[i],), tabs_p, tabs_pt, 256, False, n_p)
            (q_s,) = _qkv_dec(xs, w_q_b[i], tabs_s, False)
        o, o_s = _attention(qt, kb, vt, pt_flat, q_s.reshape(n_s, 1, QK_DIM), kn, vn, ckt, cv,
                            lams, subln_g, i, lam_init, n_pages)
        x = post(x, o.reshape(n_p * t_p, VO_DIM), w_o_b[i], l, 256)
        xs = post(xs, o_s.reshape(n_s, VO_DIM).astype(BF), w_o_b[i], l, n_s)
    y_prompt = x.reshape(n_p, t_p, D_MODEL)

    return (y_prompt,
            xs.reshape(n_s, 1, D_MODEL),
            k_p,
            v_p.reshape(n_p, t_p, N_HEADS, V_DIM),
            jnp.stack(conv_p),
            k_s.reshape(n_s, 1, N_HEADS, 2, HEAD_DIM),
            v_s.reshape(n_s, 1, N_HEADS, V_DIM),
            jnp.stack(conv_s))
```

```python
import functools
import math

import jax
import jax.numpy as jnp
from jax import lax
from jax.experimental import pallas as pl
from jax.experimental.pallas import tpu as pltpu

D_MODEL = 1024
DEPTH = 4
N_A_LAYERS = DEPTH // 2
N_HEADS = 8
HEAD_DIM = 64
V_DIM = 2 * HEAD_DIM
QK_DIM = N_HEADS * 2 * HEAD_DIM
VO_DIM = N_HEADS * V_DIM
ROT_DIM = HEAD_DIM // 4
ROPE_THETA = 500000.0
D_FF = 2816
ALPHA = (2 * DEPTH) ** 0.25
LN_EPS = 1e-5
LANES = 128
SUBLANES = 8
NEG = -1e30
Q_SCALE_LOG2 = HEAD_DIM ** -0.5 * math.log2(math.e)
ATTN_CHUNKS = 4
ACC_ROWS = V_DIM + 16
POST_GROUPS = 2
POST_TM = 512
VMEM_LIMIT = 56 * 1024 * 1024

BF = jnp.bfloat16
F32 = jnp.float32


def _dot(a, b):
    return jnp.dot(a, b, preferred_element_type=F32)


def _dot_nt(a, b):
    return lax.dot_general(a, b, (((1,), (1,)), ((), ())), preferred_element_type=F32)


def _ln(x, g, b):
    mu = jnp.mean(x, axis=-1, keepdims=True)
    xc = x - mu
    var = jnp.mean(xc * xc, axis=-1, keepdims=True)
    return xc * lax.rsqrt(var + LN_EPS) * g + b


def _lam(lq1, lk1, lq2, lk2, i, lam_init):
    a = jnp.sum(lq1[i:i + 1, :] * lk1[i:i + 1, :], axis=-1, keepdims=True)
    b = jnp.sum(lq2[i:i + 1, :] * lk2[i:i + 1, :], axis=-1, keepdims=True)
    return jnp.exp(a) - jnp.exp(b) + lam_init


def _const_spec(shape):
    nd = len(shape)
    return pl.BlockSpec(shape, lambda *_: (0,) * nd, pipeline_mode=pl.Buffered(1))


def _params(sem):
    return pltpu.CompilerParams(dimension_semantics=sem, vmem_limit_bytes=VMEM_LIMIT)


def _mixer_kernel(x_ref, win_ref, cw_ref, a_ref, st_ref, carry_ref):
    @pl.when(pl.program_id(1) == 0)
    def _():
        carry_ref[...] = jnp.zeros_like(carry_ref)

    z = _dot(x_ref[0].astype(BF), win_ref[...])
    b = z[:, :D_MODEL]
    u = z[:, D_MODEL:2 * D_MODEL] * z[:, 2 * D_MODEL:]
    tm = u.shape[0]
    prev = carry_ref[...]
    row = lax.broadcasted_iota(jnp.int32, (SUBLANES, D_MODEL), 0)

    def shifted(k):
        us = pltpu.roll(u, k, 0)
        head = jnp.where(row < k, pltpu.roll(prev, k, 0), us[:SUBLANES])
        return jnp.concatenate([head, us[SUBLANES:]], axis=0)

    conv = cw_ref[0:1, :] * shifted(2) + cw_ref[1:2, :] * shifted(1) + cw_ref[2:3, :] * u
    a_ref[0] = (b * conv).astype(BF)
    last = u[tm - SUBLANES:]
    carry_ref[...] = last
    st_ref[0] = last


def _mixer_prompt(x, w_in, cw, tm=512):
    n, t, _ = x.shape
    return pl.pallas_call(
        _mixer_kernel,
        grid=(n, t // tm),
        in_specs=[pl.BlockSpec((1, tm, D_MODEL), lambda i, j: (i, j, 0)),
                  _const_spec((D_MODEL, 3 * D_MODEL)),
                  _const_spec((3, D_MODEL))],
        out_specs=[pl.BlockSpec((1, tm, D_MODEL), lambda i, j: (i, j, 0)),
                   pl.BlockSpec((1, SUBLANES, D_MODEL), lambda i, j: (i, 0, 0))],
        out_shape=[jax.ShapeDtypeStruct((n, t, D_MODEL), BF),
                   jax.ShapeDtypeStruct((n, SUBLANES, D_MODEL), F32)],
        scratch_shapes=[pltpu.VMEM((SUBLANES, D_MODEL), F32)],
        compiler_params=_params(("arbitrary", "arbitrary")),
        name="mixer_prompt",
    )(x, w_in, cw)


def _mixer_dec_kernel(x_ref, s0_ref, s1_ref, win_ref, cw_ref, a_ref, u_ref):
    z = _dot(x_ref[...].astype(BF), win_ref[...])
    b = z[:, :D_MODEL]
    u = z[:, D_MODEL:2 * D_MODEL] * z[:, 2 * D_MODEL:]
    conv = cw_ref[0:1, :] * s0_ref[...] + cw_ref[1:2, :] * s1_ref[...] + cw_ref[2:3, :] * u
    a_ref[...] = (b * conv).astype(BF)
    u_ref[...] = u


def _mixer_dec(x, s0, s1, w_in, cw):
    r = x.shape[0]
    row = pl.BlockSpec((r, D_MODEL), lambda i: (0, 0))
    return pl.pallas_call(
        _mixer_dec_kernel,
        grid=(1,),
        in_specs=[row, row, row, _const_spec((D_MODEL, 3 * D_MODEL)), _const_spec((3, D_MODEL))],
        out_specs=[row, row],
        out_shape=[jax.ShapeDtypeStruct((r, D_MODEL), BF), jax.ShapeDtypeStruct((r, D_MODEL), F32)],
        compiler_params=_params(("arbitrary",)),
        name="mixer_dec",
    )(x, s0, s1, w_in, cw)


def _post_kernel(x_ref, a_ref, w_ref, g1_ref, b1_ref, wg_ref, wu_ref, wd_ref, g2_ref, b2_ref,
                 o_ref):
    tm = x_ref.shape[0]
    sm = tm // POST_GROUPS if tm % (POST_GROUPS * SUBLANES) == 0 else tm
    rows = [slice(r, r + sm) for r in range(0, tm, sm)]
    y = [_dot(a_ref[rs, :], w_ref[...]) for rs in rows]
    x1 = [_ln(ALPHA * x_ref[rs, :] + yr, g1_ref[...], b1_ref[...]) for rs, yr in zip(rows, y)]
    x1b = [v.astype(BF) for v in x1]
    gate = [_dot(v, wg_ref[...]) for v in x1b]
    up = [_dot(v, wu_ref[...]) for v in x1b]
    hh = [(jax.nn.silu(gv) * uv).astype(BF) for gv, uv in zip(gate, up)]
    down = [_dot(v, wd_ref[...]) for v in hh]
    for rs, xr, dr in zip(rows, x1, down):
        o_ref[rs, :] = _ln(ALPHA * xr + dr, g2_ref[...], b2_ref[...])


def _post(x, a, w, g1, b1, wg, wu, wd, g2, b2, tm):
    r = x.shape[0]
    row = pl.BlockSpec((tm, D_MODEL), lambda i: (i, 0))
    vec = _const_spec((1, D_MODEL))
    return pl.pallas_call(
        _post_kernel,
        grid=(r // tm,),
        in_specs=[row, row, _const_spec((D_MODEL, D_MODEL)), vec, vec,
                  _const_spec((D_MODEL, D_FF)), _const_spec((D_MODEL, D_FF)),
                  _const_spec((D_FF, D_MODEL)), vec, vec],
        out_specs=row,
        out_shape=jax.ShapeDtypeStruct((r, D_MODEL), F32),
        compiler_params=_params(("arbitrary",)),
        name="post",
    )(x, a, w, g1, b1, wg, wu, wd, g2, b2)


def _rope(z, c, sa, sb):
    parts = []
    for j in range(z.shape[1] // LANES):
        zj = z[:, j * LANES:(j + 1) * LANES]
        parts.append(zj * c + pltpu.roll(zj, LANES - ROT_DIM // 2, 1) * sa
                     + pltpu.roll(zj, ROT_DIM // 2, 1) * sb)
    return jnp.concatenate(parts, axis=1)


def _rope_t(zt, ct, st):
    half = ROT_DIM // 2
    tiles = []
    for g in range(zt.shape[0] // HEAD_DIM):
        base = g * HEAD_DIM
        x1 = zt[base:base + half]
        x2 = zt[base + half:base + ROT_DIM]
        tiles += [x1 * ct - x2 * st, x2 * ct + x1 * st, zt[base + ROT_DIM:base + HEAD_DIM]]
    return jnp.concatenate(tiles, axis=0)


def _qkv_prompt_kernel(*refs, with_kv):
    if with_kv:
        (x_ref, wk_ref, wkt_ref, wv_ref, wvt_ref, wqt_ref, c_ref, sa_ref, sb_ref, ct_ref, st_ref,
         kt_ref, v_ref, kb_ref, vt_ref, qt_ref) = refs
    else:
        x_ref, wqt_ref, ct_ref, st_ref, qt_ref = refs
    xb = x_ref[...].astype(BF)
    if with_kv:
        kt_ref[0] = _rope_t(_dot_nt(wkt_ref[...], xb), ct_ref[...], st_ref[...])
        kb = _rope(_dot(xb, wk_ref[...]), c_ref[...], sa_ref[...], sb_ref[...]).astype(BF)
        v_ref[...] = _dot(xb, wv_ref[...])
        vt = _dot_nt(wvt_ref[...], xb).astype(BF)
        for h in range(N_HEADS):
            kb_ref[0, h] = kb[:, h * V_DIM:(h + 1) * V_DIM]
            vt_ref[0, h] = vt[h * V_DIM:(h + 1) * V_DIM, :]
    qt = _rope_t(_dot_nt(wqt_ref[...], xb), ct_ref[...], st_ref[...]) * Q_SCALE_LOG2
    qt = qt.astype(BF)
    for h in range(N_HEADS):
        qt_ref[0, h] = qt[h * V_DIM:(h + 1) * V_DIM, :]


def _qkv_prompt(x, ws, tabs, tabs_t, tm, with_kv, n_batch):
    r = x.shape[0]
    t = r // n_batch
    nt = t // tm
    row = pl.BlockSpec((tm, D_MODEL), lambda i: (i, 0))
    wspec = _const_spec((D_MODEL, D_MODEL))
    tab = pl.BlockSpec((tm, LANES), lambda i: (i % nt, 0))
    tab_t = pl.BlockSpec((SUBLANES, tm), lambda i: (0, i % nt))
    nat = pl.BlockSpec((1, N_HEADS, tm, V_DIM), lambda i: (i // nt, 0, i % nt, 0))
    tr = pl.BlockSpec((1, N_HEADS, V_DIM, tm), lambda i: (i // nt, 0, 0, i % nt))
    nat_shape = jax.ShapeDtypeStruct((n_batch, N_HEADS, t, V_DIM), BF)
    tr_shape = jax.ShapeDtypeStruct((n_batch, N_HEADS, V_DIM, t), BF)
    f32_shape = jax.ShapeDtypeStruct((r, D_MODEL), F32)
    if with_kv:
        in_specs = [row, wspec, wspec, wspec, wspec, wspec, tab, tab, tab, tab_t, tab_t]
        args = (x, *ws, *tabs, *tabs_t)
        out_specs = [pl.BlockSpec((1, QK_DIM, tm), lambda i: (i // nt, 0, i % nt)),
                     row, nat, tr, tr]
        out_shape = [jax.ShapeDtypeStruct((n_batch, QK_DIM, t), F32),
                     f32_shape, nat_shape, tr_shape, tr_shape]
    else:
        in_specs = [row, wspec, tab_t, tab_t]
        args = (x, *ws, *tabs_t)
        out_specs = [tr]
        out_shape = [tr_shape]
    return pl.pallas_call(
        functools.partial(_qkv_prompt_kernel, with_kv=with_kv),
        grid=(r // tm,),
        in_specs=in_specs, out_specs=out_specs, out_shape=out_shape,
        compiler_params=_params(("arbitrary",)),
        name="qkv_prompt",
    )(*args)


def _qkv_dec_kernel(x_ref, w_ref, c_ref, sa_ref, sb_ref, *out_refs, with_kv):
    z = _dot(x_ref[...].astype(BF), w_ref[...])
    c, sa, sb = c_ref[...], sa_ref[...], sb_ref[...]
    if with_kv:
        k_ref, v_ref, q_ref = out_refs
        k_ref[...] = _rope(z[:, :QK_DIM], c, sa, sb)
        v_ref[...] = z[:, QK_DIM:QK_DIM + VO_DIM]
        zq = z[:, QK_DIM + VO_DIM:]
    else:
        (q_ref,) = out_refs
        zq = z
    q_ref[...] = (_rope(zq, c, sa, sb) * (HEAD_DIM ** -0.5)).astype(BF)


def _qkv_dec(x, w, tabs, with_kv):
    r = x.shape[0]
    row = pl.BlockSpec((r, D_MODEL), lambda i: (0, 0))
    tab = pl.BlockSpec((r, LANES), lambda i: (0, 0))
    f32_shape = jax.ShapeDtypeStruct((r, D_MODEL), F32)
    q_shape = jax.ShapeDtypeStruct((r, D_MODEL), BF)
    return pl.pallas_call(
        functools.partial(_qkv_dec_kernel, with_kv=with_kv),
        grid=(1,),
        in_specs=[row, _const_spec(w.shape), tab, tab, tab],
        out_specs=[row, row, row] if with_kv else [row],
        out_shape=[f32_shape, f32_shape, q_shape] if with_kv else [q_shape],
        compiler_params=_params(("arbitrary",)),
        name="qkv_dec",
    )(x, w, *tabs)


def _decode_half(half, qs_ref, kn_ref, vn_ref, k_refs, v_refs, od_ref, dm_scr, dl_scr, dacc_scr,
                 expand_scr, lam, g):
    nm = 2 * N_HEADS
    hp = len(k_refs)
    page = k_refs[0].shape[2]

    @pl.when(half == 0)
    def _():
        dm_scr[...] = jnp.full_like(dm_scr, NEG)
        dl_scr[...] = jnp.zeros_like(dl_scr)
        dacc_scr[...] = jnp.zeros_like(dacc_scr)

    grp = lax.shift_right_logical(lax.broadcasted_iota(jnp.int32, (nm, QK_DIM), 1), 6)
    r = lax.broadcasted_iota(jnp.int32, (nm, QK_DIM), 0)
    sel = r == (grp & 1) * N_HEADS + lax.shift_right_logical(grp, 1)
    qbd = jnp.where(sel, jnp.broadcast_to(qs_ref[0].astype(F32), (nm, QK_DIM)), 0.0)
    qbd_b = qbd.astype(BF)

    def pair(refs, pg):
        return jnp.concatenate([refs[pg][0].astype(BF), refs[pg + 1][0].astype(BF)], axis=1)

    s = jnp.concatenate([_dot(qbd_b, pair(k_refs, pg)) for pg in range(0, hp, 2)], axis=1)
    m_prev = dm_scr[:, :1]
    m_new = jnp.maximum(m_prev, jnp.max(s, axis=-1, keepdims=True))
    alpha = jnp.exp(m_prev - m_new)
    p = jnp.exp(s - m_new)
    l_new = alpha * dl_scr[:, :1] + jnp.sum(p, axis=-1, keepdims=True)
    pb = p.astype(BF)

    pst = jnp.concatenate([pb[:, pg * page:(pg + 1) * page] for pg in range(hp)], axis=0)
    pexp = _dot(pst, expand_scr[...])
    keep = ((lax.broadcasted_iota(jnp.int32, (2 * nm, page * N_HEADS), 1) & (N_HEADS - 1))
            == (lax.broadcasted_iota(jnp.int32, (2 * nm, page * N_HEADS), 0) & (N_HEADS - 1)))
    pv = jnp.zeros((nm, V_DIM), F32)
    for pg in range(0, hp, 2):
        p_pair = jnp.where(keep, pexp[pg * nm:(pg + 2) * nm], 0.0).astype(BF)
        o2 = _dot(p_pair, pair(v_refs, pg))
        pv = pv + o2[:nm, :V_DIM] + o2[nm:, V_DIM:]
    acc_new = alpha * dacc_scr[...] + pv
    dm_scr[...] = jnp.broadcast_to(m_new, dm_scr.shape)
    dl_scr[...] = jnp.broadcast_to(l_new, dl_scr.shape)
    dacc_scr[...] = acc_new

    @pl.when(half == 1)
    def _():
        s_tok = jnp.sum(qbd * kn_ref[0].astype(F32), axis=-1, keepdims=True)
        m_fin = jnp.maximum(m_new, s_tok)
        beta = jnp.exp(m_new - m_fin)
        p_tok = jnp.exp(s_tok - m_fin)
        l_fin = beta * l_new + p_tok
        vn2 = jnp.concatenate([vn_ref[0], vn_ref[0]], axis=0)
        on = (beta * acc_new + p_tok.astype(BF).astype(F32) * vn2) / l_fin
        o = on[:N_HEADS] - lam * on[N_HEADS:]
        ms = jnp.mean(o * o, axis=-1, keepdims=True)
        od_ref[0] = o * lax.rsqrt(ms + LN_EPS) * g


def _attn_kernel(it_ref, jt_ref, pt_ref, qt_ref, k_ref, vt_ref, lq1_ref, lk1_ref, lq2_ref,
                 lk2_ref, g_ref, qs_ref, kn_ref, vn_ref, *rest, layer_i, lam_init, tq, hp, n_units):
    del pt_ref
    kp_refs = rest[:hp]
    vp_refs = rest[hp:2 * hp]
    (o_ref, od_ref, q2t_scr, m_scr, acc_scr, bias_scr, dm_scr, dl_scr, dacc_scr,
     expand_scr) = rest[2 * hp:]
    b = pl.program_id(0)
    step_id = pl.program_id(1)
    i = it_ref[step_id]
    j = jt_ref[step_id]
    cw = 2 * tq // ATTN_CHUNKS
    cols = [slice(c * cw, (c + 1) * cw) for c in range(ATTN_CHUNKS)]
    lam = _lam(lq1_ref, lk1_ref, lq2_ref, lk2_ref, layer_i, lam_init)
    g = g_ref[layer_i:layer_i + 1, :] * (1.0 - lam_init)

    @pl.when((b == 0) & (step_id == 0))
    def _constants():
        kk = lax.broadcasted_iota(jnp.int32, (tq, 2 * tq), 0)
        qq = lax.broadcasted_iota(jnp.int32, (tq, 2 * tq), 1)
        qq = jnp.where(qq >= tq, qq - tq, qq)
        bias_scr[...] = jnp.where(kk > qq, NEG, 0.0)
        e_lane = lax.shift_right_logical(lax.broadcasted_iota(jnp.int32, expand_scr.shape, 1), 3)
        e_pos = lax.broadcasted_iota(jnp.int32, expand_scr.shape, 0)
        expand_scr[...] = jnp.where(e_lane == e_pos, 1.0, 0.0).astype(BF)

    unit = b * pl.num_programs(1) + step_id

    @pl.when(unit < n_units)
    def _decode():
        _decode_half(unit & 1, qs_ref, kn_ref, vn_ref, kp_refs, vp_refs, od_ref,
                     dm_scr, dl_scr, dacc_scr, expand_scr, lam, g)

    @pl.when(j == 0)
    def _init():
        zero = jnp.zeros((HEAD_DIM, tq), BF)
        for h in range(N_HEADS):
            q2t_scr[h, :HEAD_DIM, :tq] = qt_ref[0, h, :HEAD_DIM, :]
            q2t_scr[h, HEAD_DIM:, :tq] = zero
            q2t_scr[h, :HEAD_DIM, tq:] = zero
            q2t_scr[h, HEAD_DIM:, tq:] = qt_ref[0, h, HEAD_DIM:, :]
        m_scr[...] = jnp.full_like(m_scr, NEG)
        acc_scr[...] = jnp.zeros_like(acc_scr)

    def step(masked):
        ones = jnp.ones((ACC_ROWS - V_DIM, tq), BF)

        def scores(h, c):
            s = _dot(k_ref[0, h], q2t_scr[h, :, cols[c]])
            if masked:
                s = s + bias_scr[:, cols[c]]
            return s

        s_next = [scores(0, c) for c in range(ATTN_CHUNKS)]
        for h in range(N_HEADS):
            s_cur, s_next = s_next, []
            va = jnp.concatenate([vt_ref[0, h], ones], axis=0)
            for c in range(ATTN_CHUNKS):
                if h + 1 < N_HEADS:
                    s_next.append(scores(h + 1, c))
                s = s_cur[c]
                m_prev = m_scr[h, :, cols[c]]
                m_new = jnp.maximum(m_prev, jnp.max(s, axis=0, keepdims=True))
                alpha = jnp.exp2(m_prev - m_new)
                p = jnp.exp2(s - m_new).astype(BF)
                acc_scr[h, :, cols[c]] = alpha * acc_scr[h, :, cols[c]] + _dot(va, p)
                m_scr[h, :, cols[c]] = m_new

    @pl.when(j < i)
    def _():
        step(False)

    @pl.when(j == i)
    def _():
        step(True)
        for h in range(N_HEADS):
            acc = acc_scr[h, :V_DIM, :]
            inv = 1.0 / acc_scr[h, V_DIM:V_DIM + 1, :]
            ot = acc[:, :tq] * inv[:, :tq] - lam * (acc[:, tq:] * inv[:, tq:])
            ms = jnp.mean(ot * ot, axis=0, keepdims=True)
            o = (ot * lax.rsqrt(ms + LN_EPS)).T * g
            o_ref[0, :, h * V_DIM:(h + 1) * V_DIM] = o.astype(BF)


def _attention(qt, k, vt, pt_flat, qs, kn, vn, ckt, cv, lams, g, layer_i, lam_init, n_pages,
               tq=512):
    n, h, t, _ = k.shape
    ns = qs.shape[0]
    nq = t // tq
    pairs = [(i, j) for i in range(nq) for j in range(i + 1)]
    n_steps = len(pairs)
    it = jnp.asarray([p[0] for p in pairs], jnp.int32)
    jt = jnp.asarray([p[1] for p in pairs], jnp.int32)
    hp = n_pages // 2
    n_units = 2 * ns
    assert n_pages % 4 == 0 and n_units <= n * n_steps

    def seq_of(b, s):
        return jnp.minimum((b * n_steps + s) // 2, ns - 1)

    def page_spec(arr, pg):
        def index(b, s, it, jt, pt):
            return (pt[seq_of(b, s) * n_pages + ((b * n_steps + s) % 2) * hp + pg], 0, 0)
        return pl.BlockSpec((1,) + arr.shape[1:], index)

    qspec = pl.BlockSpec((1, h, V_DIM, tq), lambda b, s, it, jt, pt: (b, 0, 0, it[s]))
    kspec = pl.BlockSpec((1, h, tq, V_DIM), lambda b, s, it, jt, pt: (b, 0, jt[s], 0))
    vspec = pl.BlockSpec((1, h, V_DIM, tq), lambda b, s, it, jt, pt: (b, 0, 0, jt[s]))
    tok = pl.BlockSpec((1, 1, D_MODEL), lambda b, s, it, jt, pt: (seq_of(b, s), 0, 0))
    hv = pl.BlockSpec((1, N_HEADS, V_DIM), lambda b, s, it, jt, pt: (seq_of(b, s), 0, 0))
    lspec = _const_spec(lams[0].shape)
    grid_spec = pltpu.PrefetchScalarGridSpec(
        num_scalar_prefetch=3,
        grid=(n, n_steps),
        in_specs=[qspec, kspec, vspec, lspec, lspec, lspec, lspec, _const_spec(g.shape),
                  tok, tok, hv]
                 + [page_spec(ckt, pg) for pg in range(hp)]
                 + [page_spec(cv, pg) for pg in range(hp)],
        out_specs=[pl.BlockSpec((1, tq, VO_DIM), lambda b, s, it, jt, pt: (b, it[s], 0)), hv],
        scratch_shapes=[pltpu.VMEM((h, V_DIM, 2 * tq), BF),
                        pltpu.VMEM((h, 1, 2 * tq), F32),
                        pltpu.VMEM((h, ACC_ROWS, 2 * tq), F32),
                        pltpu.VMEM((tq, 2 * tq), F32),
                        pltpu.VMEM((2 * N_HEADS, LANES), F32),
                        pltpu.VMEM((2 * N_HEADS, LANES), F32),
                        pltpu.VMEM((2 * N_HEADS, V_DIM), F32),
                        pltpu.VMEM((ckt.shape[2], ckt.shape[2] * N_HEADS), BF)],
    )
    return pl.pallas_call(
        functools.partial(_attn_kernel, layer_i=layer_i, lam_init=lam_init, tq=tq, hp=hp,
                          n_units=n_units),
        grid_spec=grid_spec,
        out_shape=[jax.ShapeDtypeStruct((n, t, VO_DIM), BF),
                   jax.ShapeDtypeStruct((ns, N_HEADS, V_DIM), F32)],
        compiler_params=_params(("arbitrary", "arbitrary")),
        name="attention",
    )(it, jt, pt_flat, qt, k, vt, *lams, g, qs, kn, vn, *([ckt] * hp), *([cv] * hp))


def _rope_angles(pos):
    inv = jnp.power(ROPE_THETA, -jnp.arange(0, ROT_DIM, 2, dtype=F32) / ROT_DIM)
    ang = pos.astype(F32)[:, None] * inv[None, :]
    return jnp.cos(ang), jnp.sin(ang)


def _rope_tables(cos, sin):
    half = ROT_DIM // 2
    t = cos.shape[0]
    pad = HEAD_DIM - ROT_DIM
    c64 = jnp.concatenate([cos, cos, jnp.ones((t, pad), F32)], axis=1)
    sa64 = jnp.concatenate([-sin, jnp.zeros((t, HEAD_DIM - half), F32)], axis=1)
    sb64 = jnp.concatenate([jnp.zeros((t, half), F32), sin, jnp.zeros((t, pad), F32)], axis=1)
    return tuple(jnp.tile(x, (1, 2)) for x in (c64, sa64, sb64))


def kernel(x_prompt, x_sample, cache_k, cache_v, state_conv, page_table, w_in, conv_w, w_mix_out,
           w_kv, w_q, w_o, lambda_q1, lambda_k1, lambda_q2, lambda_k2, subln_g,
           ln1_g, ln1_b, w_gate, w_up, w_down, ln2_g, ln2_b):
    n_p, t_p, _ = x_prompt.shape
    n_s = x_sample.shape[0]
    n_pages = page_table.shape[1]
    n_pool, page = cache_k.shape[:2]
    past_len = n_pages * page

    w_in_b = w_in.astype(BF)
    w_mix_b = w_mix_out.astype(BF)
    w_k_b = w_kv[:, :QK_DIM].astype(BF)
    w_v_b = w_kv[:, QK_DIM:].astype(BF)
    w_vt_b = w_v_b.T
    w_qt_b = jnp.swapaxes(w_q, 1, 2).astype(BF)
    w_qkv_b = jnp.concatenate([w_kv, w_q[0]], axis=1).astype(BF)
    w_q_b = w_q.astype(BF)
    w_o_b = w_o.astype(BF)
    wg_b, wu_b, wd_b = w_gate.astype(BF), w_up.astype(BF), w_down.astype(BF)
    lams = (lambda_q1, lambda_k1, lambda_q2, lambda_k2)
    vec = lambda a, l: a[l].reshape(1, D_MODEL)

    def post(x, a, w, l, tm):
        return _post(x, a, w, vec(ln1_g, l), vec(ln1_b, l), wg_b[l], wu_b[l], wd_b[l],
                     vec(ln2_g, l), vec(ln2_b, l), tm)

    cos_p, sin_p = _rope_angles(jnp.arange(t_p))
    tabs_p = _rope_tables(cos_p, sin_p)
    tabs_pt = (cos_p.T, sin_p.T)
    tabs_s = tuple(jnp.broadcast_to(x, (n_s, LANES))
                   for x in _rope_tables(*_rope_angles(past_len + jnp.arange(1))))
    ckt = jnp.transpose(cache_k, (0, 2, 3, 4, 1)).reshape(n_pool, QK_DIM, page)
    cv = cache_v.reshape(n_pool, page * N_HEADS, V_DIM)
    pt_flat = page_table.reshape(-1)

    x = x_prompt
    xs = x_sample.reshape(n_s, D_MODEL)
    conv_p, conv_s = [], []
    for l in range(N_A_LAYERS):
        a, st = _mixer_prompt(x, w_in_b[l], conv_w[l])
        conv_p.append(st[:, SUBLANES - 2:, :])
        x = post(x.reshape(n_p * t_p, D_MODEL), a.reshape(n_p * t_p, D_MODEL), w_mix_b[l], l,
                 POST_TM).reshape(n_p, t_p, D_MODEL)
        a, u = _mixer_dec(xs, state_conv[l, :, 0, :], state_conv[l, :, 1, :], w_in_b[l],
                          conv_w[l])
        conv_s.append(jnp.stack([state_conv[l, :, 1, :], u], axis=1))
        xs = post(xs, a, w_mix_b[l], l, n_s)

    x = x.reshape(n_p * t_p, D_MODEL)
    kt_p, v_p, kb, vt, qt = _qkv_prompt(x, (w_k_b, w_k_b.T, w_v_b, w_vt_b, w_qt_b[0]), tabs_p,
                                         tabs_pt, 256, True, n_p)
    k_p = jnp.transpose(kt_p.reshape(n_p, N_HEADS, 2, HEAD_DIM, t_p), (0, 4, 1, 2, 3))
    k_s, v_s, q_s = _qkv_dec(xs, w_qkv_b, tabs_s, True)
    kn = k_s.astype(BF).reshape(n_s, 1, QK_DIM)
    vn = v_s.astype(BF).astype(F32).reshape(n_s, N_HEADS, V_DIM)

    for l in range(N_A_LAYERS, DEPTH):
        i = l - N_A_LAYERS
        lam_init = 0.8 - 0.6 * math.exp(-0.3 * l)
        if i > 0:
            (qt,) = _qkv_prompt(x, (w_qt_b[i],), tabs_p, tabs_pt, 256, False, n_p)
            (q_s,) = _qkv_dec(xs, w_q_b[i], tabs_s, False)
        o, o_s = _attention(qt, kb, vt, pt_flat, q_s.reshape(n_s, 1, QK_DIM), kn, vn, ckt, cv,
                            lams, subln_g, i, lam_init, n_pages)
        x = post(x, o.reshape(n_p * t_p, VO_DIM), w_o_b[i], l, POST_TM)
        xs = post(xs, o_s.reshape(n_s, VO_DIM).astype(BF), w_o_b[i], l, n_s)
    y_prompt = x.reshape(n_p, t_p, D_MODEL)

    return (y_prompt,
            xs.reshape(n_s, 1, D_MODEL),
            k_p,
            v_p.reshape(n_p, t_p, N_HEADS, V_DIM),
            jnp.stack(conv_p),
            k_s.reshape(n_s, 1, N_HEADS, 2, HEAD_DIM),
            v_s.reshape(n_s, 1, N_HEADS, V_DIM),
            jnp.stack(conv_s))
```

```python
import functools
import math

import jax
import jax.numpy as jnp
from jax import lax
from jax.experimental import pallas as pl
from jax.experimental.pallas import tpu as pltpu

D_MODEL = 1024
DEPTH = 4
N_A_LAYERS = DEPTH // 2
N_HEADS = 8
HEAD_DIM = 64
V_DIM = 2 * HEAD_DIM
QK_DIM = N_HEADS * 2 * HEAD_DIM
VO_DIM = N_HEADS * V_DIM
ROT_DIM = HEAD_DIM // 4
ROPE_THETA = 500000.0
D_FF = 2816
ALPHA = (2 * DEPTH) ** 0.25
LN_EPS = 1e-5
LANES = 128
SUBLANES = 8
NEG = -1e30
Q_SCALE_LOG2 = HEAD_DIM ** -0.5 * math.log2(math.e)
ATTN_CHUNKS = 4
ACC_ROWS = V_DIM + 16
POST_GROUPS = 2
POST_TM = 512
VMEM_LIMIT = 56 * 1024 * 1024

BF = jnp.bfloat16
F32 = jnp.float32


def _dot(a, b):
    return jnp.dot(a, b, preferred_element_type=F32)


def _dot_nt(a, b):
    return lax.dot_general(a, b, (((1,), (1,)), ((), ())), preferred_element_type=F32)


def _ln(x, g, b):
    mu = jnp.mean(x, axis=-1, keepdims=True)
    xc = x - mu
    var = jnp.mean(xc * xc, axis=-1, keepdims=True)
    return xc * lax.rsqrt(var + LN_EPS) * g + b


def _lam(lq1, lk1, lq2, lk2, i, lam_init):
    a = jnp.sum(lq1[i:i + 1, :] * lk1[i:i + 1, :], axis=-1, keepdims=True)
    b = jnp.sum(lq2[i:i + 1, :] * lk2[i:i + 1, :], axis=-1, keepdims=True)
    return jnp.exp(a) - jnp.exp(b) + lam_init


def _const_spec(shape):
    nd = len(shape)
    return pl.BlockSpec(shape, lambda *_: (0,) * nd, pipeline_mode=pl.Buffered(1))


def _params(sem):
    return pltpu.CompilerParams(dimension_semantics=sem, vmem_limit_bytes=VMEM_LIMIT)


def _mixer_kernel(x_ref, win_ref, cw_ref, a_ref, st_ref, carry_ref):
    @pl.when(pl.program_id(1) == 0)
    def _():
        carry_ref[...] = jnp.zeros_like(carry_ref)

    z = _dot(x_ref[0].astype(BF), win_ref[...])
    b = z[:, :D_MODEL]
    u = z[:, D_MODEL:2 * D_MODEL] * z[:, 2 * D_MODEL:]
    tm = u.shape[0]
    prev = carry_ref[...]
    row = lax.broadcasted_iota(jnp.int32, (SUBLANES, D_MODEL), 0)

    def shifted(k):
        us = pltpu.roll(u, k, 0)
        head = jnp.where(row < k, pltpu.roll(prev, k, 0), us[:SUBLANES])
        return jnp.concatenate([head, us[SUBLANES:]], axis=0)

    conv = cw_ref[0:1, :] * shifted(2) + cw_ref[1:2, :] * shifted(1) + cw_ref[2:3, :] * u
    a_ref[0] = (b * conv).astype(BF)
    last = u[tm - SUBLANES:]
    carry_ref[...] = last
    st_ref[0] = last


def _mixer_prompt(x, w_in, cw, tm=512):
    n, t, _ = x.shape
    return pl.pallas_call(
        _mixer_kernel,
        grid=(n, t // tm),
        in_specs=[pl.BlockSpec((1, tm, D_MODEL), lambda i, j: (i, j, 0)),
                  _const_spec((D_MODEL, 3 * D_MODEL)),
                  _const_spec((3, D_MODEL))],
        out_specs=[pl.BlockSpec((1, tm, D_MODEL), lambda i, j: (i, j, 0)),
                   pl.BlockSpec((1, SUBLANES, D_MODEL), lambda i, j: (i, 0, 0))],
        out_shape=[jax.ShapeDtypeStruct((n, t, D_MODEL), BF),
                   jax.ShapeDtypeStruct((n, SUBLANES, D_MODEL), F32)],
        scratch_shapes=[pltpu.VMEM((SUBLANES, D_MODEL), F32)],
        compiler_params=_params(("arbitrary", "arbitrary")),
        name="mixer_prompt",
    )(x, w_in, cw)


def _mixer_dec_kernel(x_ref, s0_ref, s1_ref, win_ref, cw_ref, a_ref, u_ref):
    z = _dot(x_ref[...].astype(BF), win_ref[...])
    b = z[:, :D_MODEL]
    u = z[:, D_MODEL:2 * D_MODEL] * z[:, 2 * D_MODEL:]
    conv = cw_ref[0:1, :] * s0_ref[...] + cw_ref[1:2, :] * s1_ref[...] + cw_ref[2:3, :] * u
    a_ref[...] = (b * conv).astype(BF)
    u_ref[...] = u


def _mixer_dec(x, s0, s1, w_in, cw):
    r = x.shape[0]
    row = pl.BlockSpec((r, D_MODEL), lambda i: (0, 0))
    return pl.pallas_call(
        _mixer_dec_kernel,
        grid=(1,),
        in_specs=[row, row, row, _const_spec((D_MODEL, 3 * D_MODEL)), _const_spec((3, D_MODEL))],
        out_specs=[row, row],
        out_shape=[jax.ShapeDtypeStruct((r, D_MODEL), BF), jax.ShapeDtypeStruct((r, D_MODEL), F32)],
        compiler_params=_params(("arbitrary",)),
        name="mixer_dec",
    )(x, s0, s1, w_in, cw)


def _post_rows(x_ref, a_ref, w_ref, g1_ref, b1_ref, wg_ref, wu_ref, wd_ref, g2_ref, b2_ref, o_ref):
    tm = x_ref.shape[0]
    sm = tm // POST_GROUPS if tm % (POST_GROUPS * SUBLANES) == 0 else tm
    rows = [slice(r, r + sm) for r in range(0, tm, sm)]
    y = [_dot(a_ref[rs, :], w_ref[...]) for rs in rows]
    x1 = [_ln(ALPHA * x_ref[rs, :] + yr, g1_ref[...], b1_ref[...]) for rs, yr in zip(rows, y)]
    x1b = [v.astype(BF) for v in x1]
    gate = [_dot(v, wg_ref[...]) for v in x1b]
    up = [_dot(v, wu_ref[...]) for v in x1b]
    hh = [(jax.nn.silu(gv) * uv).astype(BF) for gv, uv in zip(gate, up)]
    down = [_dot(v, wd_ref[...]) for v in hh]
    for rs, xr, dr in zip(rows, x1, down):
        o_ref[rs, :] = _ln(ALPHA * xr + dr, g2_ref[...], b2_ref[...])


def _post_kernel(x_ref, a_ref, xs_ref, as_ref, *refs):
    weights, (o_ref, os_ref) = refs[:-2], refs[-2:]
    last = pl.num_programs(0) - 1

    @pl.when(pl.program_id(0) < last)
    def _():
        _post_rows(x_ref, a_ref, *weights, o_ref)

    @pl.when(pl.program_id(0) == last)
    def _():
        _post_rows(xs_ref, as_ref, *weights, os_ref)


def _post(x, a, xs, a_s, w, g1, b1, wg, wu, wd, g2, b2, tm):
    r = x.shape[0]
    nt = r // tm
    row = pl.BlockSpec((tm, D_MODEL), lambda i: (jnp.minimum(i, nt - 1), 0))
    srow = pl.BlockSpec(xs.shape, lambda i: (0, 0))
    vec = _const_spec((1, D_MODEL))
    return pl.pallas_call(
        _post_kernel,
        grid=(nt + 1,),
        in_specs=[row, row, srow, srow, _const_spec((D_MODEL, D_MODEL)), vec, vec,
                  _const_spec((D_MODEL, D_FF)), _const_spec((D_MODEL, D_FF)),
                  _const_spec((D_FF, D_MODEL)), vec, vec],
        out_specs=[row, srow],
        out_shape=[jax.ShapeDtypeStruct((r, D_MODEL), F32),
                   jax.ShapeDtypeStruct(xs.shape, F32)],
        compiler_params=_params(("arbitrary",)),
        name="post",
    )(x, a, xs, a_s, w, g1, b1, wg, wu, wd, g2, b2)


def _rope(z, c, sa, sb):
    parts = []
    for j in range(z.shape[1] // LANES):
        zj = z[:, j * LANES:(j + 1) * LANES]
        parts.append(zj * c + pltpu.roll(zj, LANES - ROT_DIM // 2, 1) * sa
                     + pltpu.roll(zj, ROT_DIM // 2, 1) * sb)
    return jnp.concatenate(parts, axis=1)


def _rope_t(zt, ct, st):
    half = ROT_DIM // 2
    tiles = []
    for g in range(zt.shape[0] // HEAD_DIM):
        base = g * HEAD_DIM
        x1 = zt[base:base + half]
        x2 = zt[base + half:base + ROT_DIM]
        tiles += [x1 * ct - x2 * st, x2 * ct + x1 * st, zt[base + ROT_DIM:base + HEAD_DIM]]
    return jnp.concatenate(tiles, axis=0)


def _qkv_prompt_kernel(*refs, with_kv):
    if with_kv:
        (x_ref, wk_ref, wkt_ref, wv_ref, wvt_ref, wqt_ref, c_ref, sa_ref, sb_ref, ct_ref, st_ref,
         kt_ref, v_ref, kb_ref, vt_ref, qt_ref) = refs
    else:
        x_ref, wqt_ref, ct_ref, st_ref, qt_ref = refs
    xb = x_ref[...].astype(BF)
    if with_kv:
        kt_ref[0] = _rope_t(_dot_nt(wkt_ref[...], xb), ct_ref[...], st_ref[...])
        kb = _rope(_dot(xb, wk_ref[...]), c_ref[...], sa_ref[...], sb_ref[...]).astype(BF)
        v_ref[...] = _dot(xb, wv_ref[...])
        vt = _dot_nt(wvt_ref[...], xb).astype(BF)
        for h in range(N_HEADS):
            kb_ref[0, h] = kb[:, h * V_DIM:(h + 1) * V_DIM]
            vt_ref[0, h] = vt[h * V_DIM:(h + 1) * V_DIM, :]
    qt = _rope_t(_dot_nt(wqt_ref[...], xb), ct_ref[...], st_ref[...]) * Q_SCALE_LOG2
    qt = qt.astype(BF)
    for h in range(N_HEADS):
        qt_ref[0, h] = qt[h * V_DIM:(h + 1) * V_DIM, :]


def _qkv_prompt(x, ws, tabs, tabs_t, tm, with_kv, n_batch):
    r = x.shape[0]
    t = r // n_batch
    nt = t // tm
    row = pl.BlockSpec((tm, D_MODEL), lambda i: (i, 0))
    wspec = _const_spec((D_MODEL, D_MODEL))
    tab = pl.BlockSpec((tm, LANES), lambda i: (i % nt, 0))
    tab_t = pl.BlockSpec((SUBLANES, tm), lambda i: (0, i % nt))
    nat = pl.BlockSpec((1, N_HEADS, tm, V_DIM), lambda i: (i // nt, 0, i % nt, 0))
    tr = pl.BlockSpec((1, N_HEADS, V_DIM, tm), lambda i: (i // nt, 0, 0, i % nt))
    nat_shape = jax.ShapeDtypeStruct((n_batch, N_HEADS, t, V_DIM), BF)
    tr_shape = jax.ShapeDtypeStruct((n_batch, N_HEADS, V_DIM, t), BF)
    f32_shape = jax.ShapeDtypeStruct((r, D_MODEL), F32)
    if with_kv:
        in_specs = [row, wspec, wspec, wspec, wspec, wspec, tab, tab, tab, tab_t, tab_t]
        args = (x, *ws, *tabs, *tabs_t)
        out_specs = [pl.BlockSpec((1, QK_DIM, tm), lambda i: (i // nt, 0, i % nt)),
                     row, nat, tr, tr]
        out_shape = [jax.ShapeDtypeStruct((n_batch, QK_DIM, t), F32),
                     f32_shape, nat_shape, tr_shape, tr_shape]
    else:
        in_specs = [row, wspec, tab_t, tab_t]
        args = (x, *ws, *tabs_t)
        out_specs = [tr]
        out_shape = [tr_shape]
    return pl.pallas_call(
        functools.partial(_qkv_prompt_kernel, with_kv=with_kv),
        grid=(r // tm,),
        in_specs=in_specs, out_specs=out_specs, out_shape=out_shape,
        compiler_params=_params(("arbitrary",)),
        name="qkv_prompt",
    )(*args)


def _qkv_dec_kernel(x_ref, w_ref, c_ref, sa_ref, sb_ref, *out_refs, with_kv):
    z = _dot(x_ref[...].astype(BF), w_ref[...])
    c, sa, sb = c_ref[...], sa_ref[...], sb_ref[...]
    if with_kv:
        k_ref, v_ref, q_ref = out_refs
        k_ref[...] = _rope(z[:, :QK_DIM], c, sa, sb)
        v_ref[...] = z[:, QK_DIM:QK_DIM + VO_DIM]
        zq = z[:, QK_DIM + VO_DIM:]
    else:
        (q_ref,) = out_refs
        zq = z
    q_ref[...] = (_rope(zq, c, sa, sb) * (HEAD_DIM ** -0.5)).astype(BF)


def _qkv_dec(x, w, tabs, with_kv):
    r = x.shape[0]
    row = pl.BlockSpec((r, D_MODEL), lambda i: (0, 0))
    tab = pl.BlockSpec((r, LANES), lambda i: (0, 0))
    f32_shape = jax.ShapeDtypeStruct((r, D_MODEL), F32)
    q_shape = jax.ShapeDtypeStruct((r, D_MODEL), BF)
    return pl.pallas_call(
        functools.partial(_qkv_dec_kernel, with_kv=with_kv),
        grid=(1,),
        in_specs=[row, _const_spec(w.shape), tab, tab, tab],
        out_specs=[row, row, row] if with_kv else [row],
        out_shape=[f32_shape, f32_shape, q_shape] if with_kv else [q_shape],
        compiler_params=_params(("arbitrary",)),
        name="qkv_dec",
    )(x, w, *tabs)


def _decode_half(half, qs_ref, kn_ref, vn_ref, k_refs, v_refs, od_ref, dm_scr, dl_scr, dacc_scr,
                 expand_scr, lam, g):
    nm = 2 * N_HEADS
    hp = len(k_refs)
    page = k_refs[0].shape[2]

    @pl.when(half == 0)
    def _():
        dm_scr[...] = jnp.full_like(dm_scr, NEG)
        dl_scr[...] = jnp.zeros_like(dl_scr)
        dacc_scr[...] = jnp.zeros_like(dacc_scr)

    grp = lax.shift_right_logical(lax.broadcasted_iota(jnp.int32, (nm, QK_DIM), 1), 6)
    r = lax.broadcasted_iota(jnp.int32, (nm, QK_DIM), 0)
    sel = r == (grp & 1) * N_HEADS + lax.shift_right_logical(grp, 1)
    qbd = jnp.where(sel, jnp.broadcast_to(qs_ref[0].astype(F32), (nm, QK_DIM)), 0.0)
    qbd_b = qbd.astype(BF)

    def pair(refs, pg):
        return jnp.concatenate([refs[pg][0].astype(BF), refs[pg + 1][0].astype(BF)], axis=1)

    s = jnp.concatenate([_dot(qbd_b, pair(k_refs, pg)) for pg in range(0, hp, 2)], axis=1)
    m_prev = dm_scr[:, :1]
    m_new = jnp.maximum(m_prev, jnp.max(s, axis=-1, keepdims=True))
    alpha = jnp.exp(m_prev - m_new)
    p = jnp.exp(s - m_new)
    l_new = alpha * dl_scr[:, :1] + jnp.sum(p, axis=-1, keepdims=True)
    pb = p.astype(BF)

    pst = jnp.concatenate([pb[:, pg * page:(pg + 1) * page] for pg in range(hp)], axis=0)
    pexp = _dot(pst, expand_scr[...])
    keep = ((lax.broadcasted_iota(jnp.int32, (2 * nm, page * N_HEADS), 1) & (N_HEADS - 1))
            == (lax.broadcasted_iota(jnp.int32, (2 * nm, page * N_HEADS), 0) & (N_HEADS - 1)))
    pv = jnp.zeros((nm, V_DIM), F32)
    for pg in range(0, hp, 2):
        p_pair = jnp.where(keep, pexp[pg * nm:(pg + 2) * nm], 0.0).astype(BF)
        o2 = _dot(p_pair, pair(v_refs, pg))
        pv = pv + o2[:nm, :V_DIM] + o2[nm:, V_DIM:]
    acc_new = alpha * dacc_scr[...] + pv
    dm_scr[...] = jnp.broadcast_to(m_new, dm_scr.shape)
    dl_scr[...] = jnp.broadcast_to(l_new, dl_scr.shape)
    dacc_scr[...] = acc_new

    @pl.when(half == 1)
    def _():
        s_tok = jnp.sum(qbd * kn_ref[0].astype(F32), axis=-1, keepdims=True)
        m_fin = jnp.maximum(m_new, s_tok)
        beta = jnp.exp(m_new - m_fin)
        p_tok = jnp.exp(s_tok - m_fin)
        l_fin = beta * l_new + p_tok
        vn2 = jnp.concatenate([vn_ref[0], vn_ref[0]], axis=0)
        on = (beta * acc_new + p_tok.astype(BF).astype(F32) * vn2) / l_fin
        o = on[:N_HEADS] - lam * on[N_HEADS:]
        ms = jnp.mean(o * o, axis=-1, keepdims=True)
        od_ref[0] = o * lax.rsqrt(ms + LN_EPS) * g


def _attn_kernel(it_ref, jt_ref, pt_ref, qt_ref, k_ref, vt_ref, lq1_ref, lk1_ref, lq2_ref,
                 lk2_ref, g_ref, qs_ref, kn_ref, vn_ref, *rest, layer_i, lam_init, tq, hp, n_units):
    del pt_ref
    kp_refs = rest[:hp]
    vp_refs = rest[hp:2 * hp]
    (o_ref, od_ref, q2t_scr, m_scr, acc_scr, bias_scr, dm_scr, dl_scr, dacc_scr,
     expand_scr) = rest[2 * hp:]
    b = pl.program_id(0)
    step_id = pl.program_id(1)
    i = it_ref[step_id]
    j = jt_ref[step_id]
    cw = 2 * tq // ATTN_CHUNKS
    cols = [slice(c * cw, (c + 1) * cw) for c in range(ATTN_CHUNKS)]
    lam = _lam(lq1_ref, lk1_ref, lq2_ref, lk2_ref, layer_i, lam_init)
    g = g_ref[layer_i:layer_i + 1, :] * (1.0 - lam_init)

    @pl.when((b == 0) & (step_id == 0))
    def _constants():
        kk = lax.broadcasted_iota(jnp.int32, (tq, 2 * tq), 0)
        qq = lax.broadcasted_iota(jnp.int32, (tq, 2 * tq), 1)
        qq = jnp.where(qq >= tq, qq - tq, qq)
        bias_scr[...] = jnp.where(kk > qq, NEG, 0.0)
        e_lane = lax.shift_right_logical(lax.broadcasted_iota(jnp.int32, expand_scr.shape, 1), 3)
        e_pos = lax.broadcasted_iota(jnp.int32, expand_scr.shape, 0)
        expand_scr[...] = jnp.where(e_lane == e_pos, 1.0, 0.0).astype(BF)

    unit = b * pl.num_programs(1) + step_id

    @pl.when(unit < n_units)
    def _decode():
        _decode_half(unit & 1, qs_ref, kn_ref, vn_ref, kp_refs, vp_refs, od_ref,
                     dm_scr, dl_scr, dacc_scr, expand_scr, lam, g)

    @pl.when(j == 0)
    def _init():
        zero = jnp.zeros((HEAD_DIM, tq), BF)
        for h in range(N_HEADS):
            q2t_scr[h, :HEAD_DIM, :tq] = qt_ref[0, h, :HEAD_DIM, :]
            q2t_scr[h, HEAD_DIM:, :tq] = zero
            q2t_scr[h, :HEAD_DIM, tq:] = zero
            q2t_scr[h, HEAD_DIM:, tq:] = qt_ref[0, h, HEAD_DIM:, :]
        m_scr[...] = jnp.full_like(m_scr, NEG)
        acc_scr[...] = jnp.zeros_like(acc_scr)

    def step(masked):
        ones = jnp.ones((ACC_ROWS - V_DIM, tq), BF)

        def scores(h, c):
            s = _dot(k_ref[0, h], q2t_scr[h, :, cols[c]])
            if masked:
                s = s + bias_scr[:, cols[c]]
            return s

        s_next = [scores(0, c) for c in range(ATTN_CHUNKS)]
        for h in range(N_HEADS):
            s_cur, s_next = s_next, []
            va = jnp.concatenate([vt_ref[0, h], ones], axis=0)
            for c in range(ATTN_CHUNKS):
                if h + 1 < N_HEADS:
                    s_next.append(scores(h + 1, c))
                s = s_cur[c]
                m_prev = m_scr[h, :, cols[c]]
                m_new = jnp.maximum(m_prev, jnp.max(s, axis=0, keepdims=True))
                alpha = jnp.exp2(m_prev - m_new)
                p = jnp.exp2(s - m_new).astype(BF)
                acc_scr[h, :, cols[c]] = alpha * acc_scr[h, :, cols[c]] + _dot(va, p)
                m_scr[h, :, cols[c]] = m_new

    @pl.when(j < i)
    def _():
        step(False)

    @pl.when(j == i)
    def _():
        step(True)
        for h in range(N_HEADS):
            acc = acc_scr[h, :V_DIM, :]
            inv = 1.0 / acc_scr[h, V_DIM:V_DIM + 1, :]
            ot = acc[:, :tq] * inv[:, :tq] - lam * (acc[:, tq:] * inv[:, tq:])
            ms = jnp.mean(ot * ot, axis=0, keepdims=True)
            o = (ot * lax.rsqrt(ms + LN_EPS)).T * g
            o_ref[0, :, h * V_DIM:(h + 1) * V_DIM] = o.astype(BF)


def _attention(qt, k, vt, pt_flat, qs, kn, vn, ckt, cv, lams, g, layer_i, lam_init, n_pages,
               tq=512):
    n, h, t, _ = k.shape
    ns = qs.shape[0]
    nq = t // tq
    pairs = [(i, j) for i in range(nq) for j in range(i + 1)]
    n_steps = len(pairs)
    it = jnp.asarray([p[0] for p in pairs], jnp.int32)
    jt = jnp.asarray([p[1] for p in pairs], jnp.int32)
    hp = n_pages // 2
    n_units = 2 * ns
    assert n_pages % 4 == 0 and n_units <= n * n_steps

    def seq_of(b, s):
        return jnp.minimum((b * n_steps + s) // 2, ns - 1)

    def page_spec(arr, pg):
        def index(b, s, it, jt, pt):
            return (pt[seq_of(b, s) * n_pages + ((b * n_steps + s) % 2) * hp + pg], 0, 0)
        return pl.BlockSpec((1,) + arr.shape[1:], index)

    qspec = pl.BlockSpec((1, h, V_DIM, tq), lambda b, s, it, jt, pt: (b, 0, 0, it[s]))
    kspec = pl.BlockSpec((1, h, tq, V_DIM), lambda b, s, it, jt, pt: (b, 0, jt[s], 0))
    vspec = pl.BlockSpec((1, h, V_DIM, tq), lambda b, s, it, jt, pt: (b, 0, 0, jt[s]))
    tok = pl.BlockSpec((1, 1, D_MODEL), lambda b, s, it, jt, pt: (seq_of(b, s), 0, 0))
    hv = pl.BlockSpec((1, N_HEADS, V_DIM), lambda b, s, it, jt, pt: (seq_of(b, s), 0, 0))
    lspec = _const_spec(lams[0].shape)
    grid_spec = pltpu.PrefetchScalarGridSpec(
        num_scalar_prefetch=3,
        grid=(n, n_steps),
        in_specs=[qspec, kspec, vspec, lspec, lspec, lspec, lspec, _const_spec(g.shape),
                  tok, tok, hv]
                 + [page_spec(ckt, pg) for pg in range(hp)]
                 + [page_spec(cv, pg) for pg in range(hp)],
        out_specs=[pl.BlockSpec((1, tq, VO_DIM), lambda b, s, it, jt, pt: (b, it[s], 0)), hv],
        scratch_shapes=[pltpu.VMEM((h, V_DIM, 2 * tq), BF),
                        pltpu.VMEM((h, 1, 2 * tq), F32),
                        pltpu.VMEM((h, ACC_ROWS, 2 * tq), F32),
                        pltpu.VMEM((tq, 2 * tq), F32),
                        pltpu.VMEM((2 * N_HEADS, LANES), F32),
                        pltpu.VMEM((2 * N_HEADS, LANES), F32),
                        pltpu.VMEM((2 * N_HEADS, V_DIM), F32),
                        pltpu.VMEM((ckt.shape[2], ckt.shape[2] * N_HEADS), BF)],
    )
    return pl.pallas_call(
        functools.partial(_attn_kernel, layer_i=layer_i, lam_init=lam_init, tq=tq, hp=hp,
                          n_units=n_units),
        grid_spec=grid_spec,
        out_shape=[jax.ShapeDtypeStruct((n, t, VO_DIM), BF),
                   jax.ShapeDtypeStruct((ns, N_HEADS, V_DIM), F32)],
        compiler_params=_params(("arbitrary", "arbitrary")),
        name="attention",
    )(it, jt, pt_flat, qt, k, vt, *lams, g, qs, kn, vn, *([ckt] * hp), *([cv] * hp))


def _rope_angles(pos):
    inv = jnp.power(ROPE_THETA, -jnp.arange(0, ROT_DIM, 2, dtype=F32) / ROT_DIM)
    ang = pos.astype(F32)[:, None] * inv[None, :]
    return jnp.cos(ang), jnp.sin(ang)


def _rope_tables(cos, sin):
    half = ROT_DIM // 2
    t = cos.shape[0]
    pad = HEAD_DIM - ROT_DIM
    c64 = jnp.concatenate([cos, cos, jnp.ones((t, pad), F32)], axis=1)
    sa64 = jnp.concatenate([-sin, jnp.zeros((t, HEAD_DIM - half), F32)], axis=1)
    sb64 = jnp.concatenate([jnp.zeros((t, half), F32), sin, jnp.zeros((t, pad), F32)], axis=1)
    return tuple(jnp.tile(x, (1, 2)) for x in (c64, sa64, sb64))


def kernel(x_prompt, x_sample, cache_k, cache_v, state_conv, page_table, w_in, conv_w, w_mix_out,
           w_kv, w_q, w_o, lambda_q1, lambda_k1, lambda_q2, lambda_k2, subln_g,
           ln1_g, ln1_b, w_gate, w_up, w_down, ln2_g, ln2_b):
    n_p, t_p, _ = x_prompt.shape
    n_s = x_sample.shape[0]
    n_pages = page_table.shape[1]
    n_pool, page = cache_k.shape[:2]
    past_len = n_pages * page

    w_in_b = w_in.astype(BF)
    w_mix_b = w_mix_out.astype(BF)
    w_k_b = w_kv[:, :QK_DIM].astype(BF)
    w_v_b = w_kv[:, QK_DIM:].astype(BF)
    w_vt_b = w_v_b.T
    w_qt_b = jnp.swapaxes(w_q, 1, 2).astype(BF)
    w_qkv_b = jnp.concatenate([w_kv, w_q[0]], axis=1).astype(BF)
    w_q_b = w_q.astype(BF)
    w_o_b = w_o.astype(BF)
    wg_b, wu_b, wd_b = w_gate.astype(BF), w_up.astype(BF), w_down.astype(BF)
    lams = (lambda_q1, lambda_k1, lambda_q2, lambda_k2)
    vec = lambda a, l: a[l].reshape(1, D_MODEL)

    def post(x, a, xs, a_s, w, l):
        return _post(x, a, xs, a_s, w, vec(ln1_g, l), vec(ln1_b, l), wg_b[l], wu_b[l], wd_b[l],
                     vec(ln2_g, l), vec(ln2_b, l), POST_TM)

    cos_p, sin_p = _rope_angles(jnp.arange(t_p))
    tabs_p = _rope_tables(cos_p, sin_p)
    tabs_pt = (cos_p.T, sin_p.T)
    tabs_s = tuple(jnp.broadcast_to(x, (n_s, LANES))
                   for x in _rope_tables(*_rope_angles(past_len + jnp.arange(1))))
    ckt = jnp.transpose(cache_k, (0, 2, 3, 4, 1)).reshape(n_pool, QK_DIM, page)
    cv = cache_v.reshape(n_pool, page * N_HEADS, V_DIM)
    pt_flat = page_table.reshape(-1)

    x = x_prompt
    xs = x_sample.reshape(n_s, D_MODEL)
    conv_p, conv_s = [], []
    for l in range(N_A_LAYERS):
        a, st = _mixer_prompt(x, w_in_b[l], conv_w[l])
        conv_p.append(st[:, SUBLANES - 2:, :])
        a_s, u = _mixer_dec(xs, state_conv[l, :, 0, :], state_conv[l, :, 1, :], w_in_b[l],
                            conv_w[l])
        conv_s.append(jnp.stack([state_conv[l, :, 1, :], u], axis=1))
        x, xs = post(x.reshape(n_p * t_p, D_MODEL), a.reshape(n_p * t_p, D_MODEL), xs, a_s,
                     w_mix_b[l], l)
        x = x.reshape(n_p, t_p, D_MODEL)

    x = x.reshape(n_p * t_p, D_MODEL)
    kt_p, v_p, kb, vt, qt = _qkv_prompt(x, (w_k_b, w_k_b.T, w_v_b, w_vt_b, w_qt_b[0]), tabs_p,
                                         tabs_pt, 256, True, n_p)
    k_p = jnp.transpose(kt_p.reshape(n_p, N_HEADS, 2, HEAD_DIM, t_p), (0, 4, 1, 2, 3))
    k_s, v_s, q_s = _qkv_dec(xs, w_qkv_b, tabs_s, True)
    kn = k_s.astype(BF).reshape(n_s, 1, QK_DIM)
    vn = v_s.astype(BF).astype(F32).reshape(n_s, N_HEADS, V_DIM)

    for l in range(N_A_LAYERS, DEPTH):
        i = l - N_A_LAYERS
        lam_init = 0.8 - 0.6 * math.exp(-0.3 * l)
        if i > 0:
            (qt,) = _qkv_prompt(x, (w_qt_b[i],), tabs_p, tabs_pt, 256, False, n_p)
            (q_s,) = _qkv_dec(xs, w_q_b[i], tabs_s, False)
        o, o_s = _attention(qt, kb, vt, pt_flat, q_s.reshape(n_s, 1, QK_DIM), kn, vn, ckt, cv,
                            lams, subln_g, i, lam_init, n_pages)
        x, xs = post(x, o.reshape(n_p * t_p, VO_DIM), xs, o_s.reshape(n_s, VO_DIM).astype(BF),
                     w_o_b[i], l)
    y_prompt = x.reshape(n_p, t_p, D_MODEL)

    return (y_prompt,
            xs.reshape(n_s, 1, D_MODEL),
            k_p,
            v_p.reshape(n_p, t_p, N_HEADS, V_DIM),
            jnp.stack(conv_p),
            k_s.reshape(n_s, 1, N_HEADS, 2, HEAD_DIM),
            v_s.reshape(n_s, 1, N_HEADS, V_DIM),
            jnp.stack(conv_s))
```

```python
import functools
import math

import jax
import jax.numpy as jnp
from jax import lax
from jax.experimental import pallas as pl
from jax.experimental.pallas import tpu as pltpu

D_MODEL = 1024
DEPTH = 4
N_A_LAYERS = DEPTH // 2
N_HEADS = 8
HEAD_DIM = 64
V_DIM = 2 * HEAD_DIM
QK_DIM = N_HEADS * 2 * HEAD_DIM
VO_DIM = N_HEADS * V_DIM
ROT_DIM = HEAD_DIM // 4
ROPE_THETA = 500000.0
D_FF = 2816
ALPHA = (2 * DEPTH) ** 0.25
LN_EPS = 1e-5
LANES = 128
SUBLANES = 8
NEG = -1e30
Q_SCALE_LOG2 = HEAD_DIM ** -0.5 * math.log2(math.e)
ATTN_CHUNKS = 4
ACC_ROWS = V_DIM + 16
POST_GROUPS = 2
POST_TM = 512
VMEM_LIMIT = 56 * 1024 * 1024

BF = jnp.bfloat16
F32 = jnp.float32


def _dot(a, b):
    return jnp.dot(a, b, preferred_element_type=F32)


def _dot_nt(a, b):
    return lax.dot_general(a, b, (((1,), (1,)), ((), ())), preferred_element_type=F32)


def _ln(x, g, b):
    mu = jnp.mean(x, axis=-1, keepdims=True)
    xc = x - mu
    var = jnp.mean(xc * xc, axis=-1, keepdims=True)
    return xc * lax.rsqrt(var + LN_EPS) * g + b


def _lam(lq1, lk1, lq2, lk2, i, lam_init):
    a = jnp.sum(lq1[i:i + 1, :] * lk1[i:i + 1, :], axis=-1, keepdims=True)
    b = jnp.sum(lq2[i:i + 1, :] * lk2[i:i + 1, :], axis=-1, keepdims=True)
    return jnp.exp(a) - jnp.exp(b) + lam_init


def _const_spec(shape):
    nd = len(shape)
    return pl.BlockSpec(shape, lambda *_: (0,) * nd, pipeline_mode=pl.Buffered(1))


def _params(sem):
    return pltpu.CompilerParams(dimension_semantics=sem, vmem_limit_bytes=VMEM_LIMIT)


def _mixer_kernel(x_ref, win_ref, cw_ref, a_ref, st_ref, carry_ref):
    @pl.when(pl.program_id(1) == 0)
    def _():
        carry_ref[...] = jnp.zeros_like(carry_ref)

    tm = x_ref.shape[1]
    zs = [_dot(x_ref[0, r:r + tm // 2, :].astype(BF), win_ref[...]) for r in (0, tm // 2)]
    b = jnp.concatenate([z[:, :D_MODEL] for z in zs], axis=0)
    u = jnp.concatenate([z[:, D_MODEL:2 * D_MODEL] * z[:, 2 * D_MODEL:] for z in zs], axis=0)
    prev = carry_ref[...]
    row = lax.broadcasted_iota(jnp.int32, (SUBLANES, D_MODEL), 0)

    def shifted(k):
        us = pltpu.roll(u, k, 0)
        head = jnp.where(row < k, pltpu.roll(prev, k, 0), us[:SUBLANES])
        return jnp.concatenate([head, us[SUBLANES:]], axis=0)

    conv = cw_ref[0:1, :] * shifted(2) + cw_ref[1:2, :] * shifted(1) + cw_ref[2:3, :] * u
    a_ref[0] = (b * conv).astype(BF)
    last = u[tm - SUBLANES:]
    carry_ref[...] = last
    st_ref[0] = last


def _mixer_prompt(x, w_in, cw, tm=512):
    n, t, _ = x.shape
    return pl.pallas_call(
        _mixer_kernel,
        grid=(n, t // tm),
        in_specs=[pl.BlockSpec((1, tm, D_MODEL), lambda i, j: (i, j, 0)),
                  _const_spec((D_MODEL, 3 * D_MODEL)),
                  _const_spec((3, D_MODEL))],
        out_specs=[pl.BlockSpec((1, tm, D_MODEL), lambda i, j: (i, j, 0)),
                   pl.BlockSpec((1, SUBLANES, D_MODEL), lambda i, j: (i, 0, 0))],
        out_shape=[jax.ShapeDtypeStruct((n, t, D_MODEL), BF),
                   jax.ShapeDtypeStruct((n, SUBLANES, D_MODEL), F32)],
        scratch_shapes=[pltpu.VMEM((SUBLANES, D_MODEL), F32)],
        compiler_params=_params(("arbitrary", "arbitrary")),
        name="mixer_prompt",
    )(x, w_in, cw)


def _mixer_dec_kernel(x_ref, s0_ref, s1_ref, win_ref, cw_ref, a_ref, u_ref):
    z = _dot(x_ref[...].astype(BF), win_ref[...])
    b = z[:, :D_MODEL]
    u = z[:, D_MODEL:2 * D_MODEL] * z[:, 2 * D_MODEL:]
    conv = cw_ref[0:1, :] * s0_ref[...] + cw_ref[1:2, :] * s1_ref[...] + cw_ref[2:3, :] * u
    a_ref[...] = (b * conv).astype(BF)
    u_ref[...] = u


def _mixer_dec(x, s0, s1, w_in, cw):
    r = x.shape[0]
    row = pl.BlockSpec((r, D_MODEL), lambda i: (0, 0))
    return pl.pallas_call(
        _mixer_dec_kernel,
        grid=(1,),
        in_specs=[row, row, row, _const_spec((D_MODEL, 3 * D_MODEL)), _const_spec((3, D_MODEL))],
        out_specs=[row, row],
        out_shape=[jax.ShapeDtypeStruct((r, D_MODEL), BF), jax.ShapeDtypeStruct((r, D_MODEL), F32)],
        compiler_params=_params(("arbitrary",)),
        name="mixer_dec",
    )(x, s0, s1, w_in, cw)


def _post_rows(x_ref, a_ref, w_ref, g1_ref, b1_ref, wg_ref, wu_ref, wd_ref, g2_ref, b2_ref, o_ref):
    tm = x_ref.shape[0]
    sm = tm // POST_GROUPS if tm % (POST_GROUPS * SUBLANES) == 0 else tm
    rows = [slice(r, r + sm) for r in range(0, tm, sm)]
    y = [_dot(a_ref[rs, :], w_ref[...]) for rs in rows]
    x1 = [_ln(ALPHA * x_ref[rs, :] + yr, g1_ref[...], b1_ref[...]) for rs, yr in zip(rows, y)]
    x1b = [v.astype(BF) for v in x1]
    gate = [_dot(v, wg_ref[...]) for v in x1b]
    up = [_dot(v, wu_ref[...]) for v in x1b]
    hh = [(jax.nn.silu(gv) * uv).astype(BF) for gv, uv in zip(gate, up)]
    down = [_dot(v, wd_ref[...]) for v in hh]
    for rs, xr, dr in zip(rows, x1, down):
        o_ref[rs, :] = _ln(ALPHA * xr + dr, g2_ref[...], b2_ref[...])


def _post_kernel(x_ref, a_ref, xs_ref, as_ref, *refs):
    weights, (o_ref, os_ref) = refs[:-2], refs[-2:]
    last = pl.num_programs(0) - 1

    @pl.when(pl.program_id(0) < last)
    def _():
        _post_rows(x_ref, a_ref, *weights, o_ref)

    @pl.when(pl.program_id(0) == last)
    def _():
        _post_rows(xs_ref, as_ref, *weights, os_ref)


def _post(x, a, xs, a_s, w, g1, b1, wg, wu, wd, g2, b2, tm):
    r = x.shape[0]
    nt = r // tm
    row = pl.BlockSpec((tm, D_MODEL), lambda i: (jnp.minimum(i, nt - 1), 0))
    srow = pl.BlockSpec(xs.shape, lambda i: (0, 0))
    vec = _const_spec((1, D_MODEL))
    return pl.pallas_call(
        _post_kernel,
        grid=(nt + 1,),
        in_specs=[row, row, srow, srow, _const_spec((D_MODEL, D_MODEL)), vec, vec,
                  _const_spec((D_MODEL, D_FF)), _const_spec((D_MODEL, D_FF)),
                  _const_spec((D_FF, D_MODEL)), vec, vec],
        out_specs=[row, srow],
        out_shape=[jax.ShapeDtypeStruct((r, D_MODEL), F32),
                   jax.ShapeDtypeStruct(xs.shape, F32)],
        compiler_params=_params(("arbitrary",)),
        name="post",
    )(x, a, xs, a_s, w, g1, b1, wg, wu, wd, g2, b2)


def _rope(z, c, sa, sb):
    parts = []
    for j in range(z.shape[1] // LANES):
        zj = z[:, j * LANES:(j + 1) * LANES]
        parts.append(zj * c + pltpu.roll(zj, LANES - ROT_DIM // 2, 1) * sa
                     + pltpu.roll(zj, ROT_DIM // 2, 1) * sb)
    return jnp.concatenate(parts, axis=1)


def _rope_t(zt, ct, st):
    half = ROT_DIM // 2
    tiles = []
    for g in range(zt.shape[0] // HEAD_DIM):
        base = g * HEAD_DIM
        x1 = zt[base:base + half]
        x2 = zt[base + half:base + ROT_DIM]
        tiles += [x1 * ct - x2 * st, x2 * ct + x1 * st, zt[base + ROT_DIM:base + HEAD_DIM]]
    return jnp.concatenate(tiles, axis=0)


def _qkv_prompt_kernel(*refs, with_kv):
    if with_kv:
        (x_ref, wk_ref, wkt_ref, wv_ref, wvt_ref, wqt_ref, c_ref, sa_ref, sb_ref, ct_ref, st_ref,
         kt_ref, v_ref, kb_ref, vt_ref, qt_ref) = refs
    else:
        x_ref, wqt_ref, ct_ref, st_ref, qt_ref = refs
    xb = x_ref[...].astype(BF)
    if with_kv:
        kt_ref[0] = _rope_t(_dot_nt(wkt_ref[...], xb), ct_ref[...], st_ref[...])
        kb = _rope(_dot(xb, wk_ref[...]), c_ref[...], sa_ref[...], sb_ref[...]).astype(BF)
        v_ref[...] = _dot(xb, wv_ref[...])
        vt = _dot_nt(wvt_ref[...], xb).astype(BF)
        for h in range(N_HEADS):
            kb_ref[0, h] = kb[:, h * V_DIM:(h + 1) * V_DIM]
            vt_ref[0, h] = vt[h * V_DIM:(h + 1) * V_DIM, :]
    qt = _rope_t(_dot_nt(wqt_ref[...], xb), ct_ref[...], st_ref[...]) * Q_SCALE_LOG2
    qt = qt.astype(BF)
    for h in range(N_HEADS):
        qt_ref[0, h] = qt[h * V_DIM:(h + 1) * V_DIM, :]


def _qkv_prompt(x, ws, tabs, tabs_t, tm, with_kv, n_batch):
    r = x.shape[0]
    t = r // n_batch
    nt = t // tm
    row = pl.BlockSpec((tm, D_MODEL), lambda i: (i, 0))
    wspec = _const_spec((D_MODEL, D_MODEL))
    tab = pl.BlockSpec((tm, LANES), lambda i: (i % nt, 0))
    tab_t = pl.BlockSpec((SUBLANES, tm), lambda i: (0, i % nt))
    nat = pl.BlockSpec((1, N_HEADS, tm, V_DIM), lambda i: (i // nt, 0, i % nt, 0))
    tr = pl.BlockSpec((1, N_HEADS, V_DIM, tm), lambda i: (i // nt, 0, 0, i % nt))
    nat_shape = jax.ShapeDtypeStruct((n_batch, N_HEADS, t, V_DIM), BF)
    tr_shape = jax.ShapeDtypeStruct((n_batch, N_HEADS, V_DIM, t), BF)
    f32_shape = jax.ShapeDtypeStruct((r, D_MODEL), F32)
    if with_kv:
        in_specs = [row, wspec, wspec, wspec, wspec, wspec, tab, tab, tab, tab_t, tab_t]
        args = (x, *ws, *tabs, *tabs_t)
        out_specs = [pl.BlockSpec((1, QK_DIM, tm), lambda i: (i // nt, 0, i % nt)),
                     row, nat, tr, tr]
        out_shape = [jax.ShapeDtypeStruct((n_batch, QK_DIM, t), F32),
                     f32_shape, nat_shape, tr_shape, tr_shape]
    else:
        in_specs = [row, wspec, tab_t, tab_t]
        args = (x, *ws, *tabs_t)
        out_specs = [tr]
        out_shape = [tr_shape]
    return pl.pallas_call(
        functools.partial(_qkv_prompt_kernel, with_kv=with_kv),
        grid=(r // tm,),
        in_specs=in_specs, out_specs=out_specs, out_shape=out_shape,
        compiler_params=_params(("arbitrary",)),
        name="qkv_prompt",
    )(*args)


def _qkv_dec_kernel(x_ref, w_ref, c_ref, sa_ref, sb_ref, *out_refs, with_kv):
    z = _dot(x_ref[...].astype(BF), w_ref[...])
    c, sa, sb = c_ref[...], sa_ref[...], sb_ref[...]
    if with_kv:
        k_ref, v_ref, q_ref = out_refs
        k_ref[...] = _rope(z[:, :QK_DIM], c, sa, sb)
        v_ref[...] = z[:, QK_DIM:QK_DIM + VO_DIM]
        zq = z[:, QK_DIM + VO_DIM:]
    else:
        (q_ref,) = out_refs
        zq = z
    q_ref[...] = (_rope(zq, c, sa, sb) * (HEAD_DIM ** -0.5)).astype(BF)


def _qkv_dec(x, w, tabs, with_kv):
    r = x.shape[0]
    row = pl.BlockSpec((r, D_MODEL), lambda i: (0, 0))
    tab = pl.BlockSpec((r, LANES), lambda i: (0, 0))
    f32_shape = jax.ShapeDtypeStruct((r, D_MODEL), F32)
    q_shape = jax.ShapeDtypeStruct((r, D_MODEL), BF)
    return pl.pallas_call(
        functools.partial(_qkv_dec_kernel, with_kv=with_kv),
        grid=(1,),
        in_specs=[row, _const_spec(w.shape), tab, tab, tab],
        out_specs=[row, row, row] if with_kv else [row],
        out_shape=[f32_shape, f32_shape, q_shape] if with_kv else [q_shape],
        compiler_params=_params(("arbitrary",)),
        name="qkv_dec",
    )(x, w, *tabs)


def _decode_half(half, qs_ref, kn_ref, vn_ref, k_refs, v_refs, od_ref, dm_scr, dl_scr, dacc_scr,
                 expand_scr, lam, g):
    nm = 2 * N_HEADS
    hp = len(k_refs)
    page = k_refs[0].shape[2]

    @pl.when(half == 0)
    def _():
        dm_scr[...] = jnp.full_like(dm_scr, NEG)
        dl_scr[...] = jnp.zeros_like(dl_scr)
        dacc_scr[...] = jnp.zeros_like(dacc_scr)

    grp = lax.shift_right_logical(lax.broadcasted_iota(jnp.int32, (nm, QK_DIM), 1), 6)
    r = lax.broadcasted_iota(jnp.int32, (nm, QK_DIM), 0)
    sel = r == (grp & 1) * N_HEADS + lax.shift_right_logical(grp, 1)
    qbd = jnp.where(sel, jnp.broadcast_to(qs_ref[0].astype(F32), (nm, QK_DIM)), 0.0)
    qbd_b = qbd.astype(BF)

    def pair(refs, pg):
        return jnp.concatenate([refs[pg][0].astype(BF), refs[pg + 1][0].astype(BF)], axis=1)

    s = jnp.concatenate([_dot(qbd_b, pair(k_refs, pg)) for pg in range(0, hp, 2)], axis=1)
    m_prev = dm_scr[:, :1]
    m_new = jnp.maximum(m_prev, jnp.max(s, axis=-1, keepdims=True))
    alpha = jnp.exp(m_prev - m_new)
    p = jnp.exp(s - m_new)
    l_new = alpha * dl_scr[:, :1] + jnp.sum(p, axis=-1, keepdims=True)
    pb = p.astype(BF)

    pst = jnp.concatenate([pb[:, pg * page:(pg + 1) * page] for pg in range(hp)], axis=0)
    pexp = _dot(pst, expand_scr[...])
    keep = ((lax.broadcasted_iota(jnp.int32, (2 * nm, page * N_HEADS), 1) & (N_HEADS - 1))
            == (lax.broadcasted_iota(jnp.int32, (2 * nm, page * N_HEADS), 0) & (N_HEADS - 1)))
    pv = jnp.zeros((nm, V_DIM), F32)
    for pg in range(0, hp, 2):
        p_pair = jnp.where(keep, pexp[pg * nm:(pg + 2) * nm], 0.0).astype(BF)
        o2 = _dot(p_pair, pair(v_refs, pg))
        pv = pv + o2[:nm, :V_DIM] + o2[nm:, V_DIM:]
    acc_new = alpha * dacc_scr[...] + pv
    dm_scr[...] = jnp.broadcast_to(m_new, dm_scr.shape)
    dl_scr[...] = jnp.broadcast_to(l_new, dl_scr.shape)
    dacc_scr[...] = acc_new

    @pl.when(half == 1)
    def _():
        s_tok = jnp.sum(qbd * kn_ref[0].astype(F32), axis=-1, keepdims=True)
        m_fin = jnp.maximum(m_new, s_tok)
        beta = jnp.exp(m_new - m_fin)
        p_tok = jnp.exp(s_tok - m_fin)
        l_fin = beta * l_new + p_tok
        vn2 = jnp.concatenate([vn_ref[0], vn_ref[0]], axis=0)
        on = (beta * acc_new + p_tok.astype(BF).astype(F32) * vn2) / l_fin
        o = on[:N_HEADS] - lam * on[N_HEADS:]
        ms = jnp.mean(o * o, axis=-1, keepdims=True)
        od_ref[0] = o * lax.rsqrt(ms + LN_EPS) * g


def _attn_kernel(it_ref, jt_ref, pt_ref, qt_ref, k_ref, vt_ref, lq1_ref, lk1_ref, lq2_ref,
                 lk2_ref, g_ref, qs_ref, kn_ref, vn_ref, *rest, layer_i, lam_init, tq, hp, n_units):
    del pt_ref
    kp_refs = rest[:hp]
    vp_refs = rest[hp:2 * hp]
    (o_ref, od_ref, q2t_scr, m_scr, acc_scr, bias_scr, dm_scr, dl_scr, dacc_scr,
     expand_scr) = rest[2 * hp:]
    b = pl.program_id(0)
    step_id = pl.program_id(1)
    i = it_ref[step_id]
    j = jt_ref[step_id]
    cw = 2 * tq // ATTN_CHUNKS
    cols = [slice(c * cw, (c + 1) * cw) for c in range(ATTN_CHUNKS)]
    lam = _lam(lq1_ref, lk1_ref, lq2_ref, lk2_ref, layer_i, lam_init)
    g = g_ref[layer_i:layer_i + 1, :] * (1.0 - lam_init)

    @pl.when((b == 0) & (step_id == 0))
    def _constants():
        kk = lax.broadcasted_iota(jnp.int32, (tq, 2 * tq), 0)
        qq = lax.broadcasted_iota(jnp.int32, (tq, 2 * tq), 1)
        qq = jnp.where(qq >= tq, qq - tq, qq)
        bias_scr[...] = jnp.where(kk > qq, NEG, 0.0)
        e_lane = lax.shift_right_logical(lax.broadcasted_iota(jnp.int32, expand_scr.shape, 1), 3)
        e_pos = lax.broadcasted_iota(jnp.int32, expand_scr.shape, 0)
        expand_scr[...] = jnp.where(e_lane == e_pos, 1.0, 0.0).astype(BF)

    unit = b * pl.num_programs(1) + step_id

    @pl.when(unit < n_units)
    def _decode():
        _decode_half(unit & 1, qs_ref, kn_ref, vn_ref, kp_refs, vp_refs, od_ref,
                     dm_scr, dl_scr, dacc_scr, expand_scr, lam, g)

    @pl.when(j == 0)
    def _init():
        zero = jnp.zeros((HEAD_DIM, tq), BF)
        for h in range(N_HEADS):
            q2t_scr[h, :HEAD_DIM, :tq] = qt_ref[0, h, :HEAD_DIM, :]
            q2t_scr[h, HEAD_DIM:, :tq] = zero
            q2t_scr[h, :HEAD_DIM, tq:] = zero
            q2t_scr[h, HEAD_DIM:, tq:] = qt_ref[0, h, HEAD_DIM:, :]
        m_scr[...] = jnp.full_like(m_scr, NEG)
        acc_scr[...] = jnp.zeros_like(acc_scr)

    def step(masked):
        ones = jnp.ones((ACC_ROWS - V_DIM, tq), BF)

        def scores(h, c):
            s = _dot(k_ref[0, h], q2t_scr[h, :, cols[c]])
            if masked:
                s = s + bias_scr[:, cols[c]]
            return s

        s_next = [scores(0, c) for c in range(ATTN_CHUNKS)]
        for h in range(N_HEADS):
            s_cur, s_next = s_next, []
            va = jnp.concatenate([vt_ref[0, h], ones], axis=0)
            for c in range(ATTN_CHUNKS):
                if h + 1 < N_HEADS:
                    s_next.append(scores(h + 1, c))
                s = s_cur[c]
                m_prev = m_scr[h, :, cols[c]]
                m_new = jnp.maximum(m_prev, jnp.max(s, axis=0, keepdims=True))
                alpha = jnp.exp2(m_prev - m_new)
                p = jnp.exp2(s - m_new).astype(BF)
                acc_scr[h, :, cols[c]] = alpha * acc_scr[h, :, cols[c]] + _dot(va, p)
                m_scr[h, :, cols[c]] = m_new

    @pl.when(j < i)
    def _():
        step(False)

    @pl.when(j == i)
    def _():
        step(True)
        for h in range(N_HEADS):
            acc = acc_scr[h, :V_DIM, :]
            inv = 1.0 / acc_scr[h, V_DIM:V_DIM + 1, :]
            ot = acc[:, :tq] * inv[:, :tq] - lam * (acc[:, tq:] * inv[:, tq:])
            ms = jnp.mean(ot * ot, axis=0, keepdims=True)
            o = (ot * lax.rsqrt(ms + LN_EPS)).T * g
            o_ref[0, :, h * V_DIM:(h + 1) * V_DIM] = o.astype(BF)


def _attention(qt, k, vt, pt_flat, qs, kn, vn, ckt, cv, lams, g, layer_i, lam_init, n_pages,
               tq=512):
    n, h, t, _ = k.shape
    ns = qs.shape[0]
    nq = t // tq
    pairs = [(i, j) for i in range(nq) for j in range(i + 1)]
    n_steps = len(pairs)
    it = jnp.asarray([p[0] for p in pairs], jnp.int32)
    jt = jnp.asarray([p[1] for p in pairs], jnp.int32)
    hp = n_pages // 2
    n_units = 2 * ns
    assert n_pages % 4 == 0 and n_units <= n * n_steps

    def seq_of(b, s):
        return jnp.minimum((b * n_steps + s) // 2, ns - 1)

    def page_spec(arr, pg):
        def index(b, s, it, jt, pt):
            return (pt[seq_of(b, s) * n_pages + ((b * n_steps + s) % 2) * hp + pg], 0, 0)
        return pl.BlockSpec((1,) + arr.shape[1:], index)

    qspec = pl.BlockSpec((1, h, V_DIM, tq), lambda b, s, it, jt, pt: (b, 0, 0, it[s]))
    kspec = pl.BlockSpec((1, h, tq, V_DIM), lambda b, s, it, jt, pt: (b, 0, jt[s], 0))
    vspec = pl.BlockSpec((1, h, V_DIM, tq), lambda b, s, it, jt, pt: (b, 0, 0, jt[s]))
    tok = pl.BlockSpec((1, 1, D_MODEL), lambda b, s, it, jt, pt: (seq_of(b, s), 0, 0))
    hv = pl.BlockSpec((1, N_HEADS, V_DIM), lambda b, s, it, jt, pt: (seq_of(b, s), 0, 0))
    lspec = _const_spec(lams[0].shape)
    grid_spec = pltpu.PrefetchScalarGridSpec(
        num_scalar_prefetch=3,
        grid=(n, n_steps),
        in_specs=[qspec, kspec, vspec, lspec, lspec, lspec, lspec, _const_spec(g.shape),
                  tok, tok, hv]
                 + [page_spec(ckt, pg) for pg in range(hp)]
                 + [page_spec(cv, pg) for pg in range(hp)],
        out_specs=[pl.BlockSpec((1, tq, VO_DIM), lambda b, s, it, jt, pt: (b, it[s], 0)), hv],
        scratch_shapes=[pltpu.VMEM((h, V_DIM, 2 * tq), BF),
                        pltpu.VMEM((h, 1, 2 * tq), F32),
                        pltpu.VMEM((h, ACC_ROWS, 2 * tq), F32),
                        pltpu.VMEM((tq, 2 * tq), F32),
                        pltpu.VMEM((2 * N_HEADS, LANES), F32),
                        pltpu.VMEM((2 * N_HEADS, LANES), F32),
                        pltpu.VMEM((2 * N_HEADS, V_DIM), F32),
                        pltpu.VMEM((ckt.shape[2], ckt.shape[2] * N_HEADS), BF)],
    )
    return pl.pallas_call(
        functools.partial(_attn_kernel, layer_i=layer_i, lam_init=lam_init, tq=tq, hp=hp,
                          n_units=n_units),
        grid_spec=grid_spec,
        out_shape=[jax.ShapeDtypeStruct((n, t, VO_DIM), BF),
                   jax.ShapeDtypeStruct((ns, N_HEADS, V_DIM), F32)],
        compiler_params=_params(("arbitrary", "arbitrary")),
        name="attention",
    )(it, jt, pt_flat, qt, k, vt, *lams, g, qs, kn, vn, *([ckt] * hp), *([cv] * hp))


def _rope_angles(pos):
    inv = jnp.power(ROPE_THETA, -jnp.arange(0, ROT_DIM, 2, dtype=F32) / ROT_DIM)
    ang = pos.astype(F32)[:, None] * inv[None, :]
    return jnp.cos(ang), jnp.sin(ang)


def _rope_tables(cos, sin):
    half = ROT_DIM // 2
    t = cos.shape[0]
    pad = HEAD_DIM - ROT_DIM
    c64 = jnp.concatenate([cos, cos, jnp.ones((t, pad), F32)], axis=1)
    sa64 = jnp.concatenate([-sin, jnp.zeros((t, HEAD_DIM - half), F32)], axis=1)
    sb64 = jnp.concatenate([jnp.zeros((t, half), F32), sin, jnp.zeros((t, pad), F32)], axis=1)
    return tuple(jnp.tile(x, (1, 2)) for x in (c64, sa64, sb64))


def kernel(x_prompt, x_sample, cache_k, cache_v, state_conv, page_table, w_in, conv_w, w_mix_out,
           w_kv, w_q, w_o, lambda_q1, lambda_k1, lambda_q2, lambda_k2, subln_g,
           ln1_g, ln1_b, w_gate, w_up, w_down, ln2_g, ln2_b):
    n_p, t_p, _ = x_prompt.shape
    n_s = x_sample.shape[0]
    n_pages = page_table.shape[1]
    n_pool, page = cache_k.shape[:2]
    past_len = n_pages * page

    w_in_b = w_in.astype(BF)
    w_mix_b = w_mix_out.astype(BF)
    w_k_b = w_kv[:, :QK_DIM].astype(BF)
    w_v_b = w_kv[:, QK_DIM:].astype(BF)
    w_vt_b = w_v_b.T
    w_qt_b = jnp.swapaxes(w_q, 1, 2).astype(BF)
    w_qkv_b = jnp.concatenate([w_kv, w_q[0]], axis=1).astype(BF)
    w_q_b = w_q.astype(BF)
    w_o_b = w_o.astype(BF)
    wg_b, wu_b, wd_b = w_gate.astype(BF), w_up.astype(BF), w_down.astype(BF)
    lams = (lambda_q1, lambda_k1, lambda_q2, lambda_k2)
    vec = lambda a, l: a[l].reshape(1, D_MODEL)

    def post(x, a, xs, a_s, w, l):
        return _post(x, a, xs, a_s, w, vec(ln1_g, l), vec(ln1_b, l), wg_b[l], wu_b[l], wd_b[l],
                     vec(ln2_g, l), vec(ln2_b, l), POST_TM)

    cos_p, sin_p = _rope_angles(jnp.arange(t_p))
    tabs_p = _rope_tables(cos_p, sin_p)
    tabs_pt = (cos_p.T, sin_p.T)
    tabs_s = tuple(jnp.broadcast_to(x, (n_s, LANES))
                   for x in _rope_tables(*_rope_angles(past_len + jnp.arange(1))))
    ckt = jnp.transpose(cache_k, (0, 2, 3, 4, 1)).reshape(n_pool, QK_DIM, page)
    cv = cache_v.reshape(n_pool, page * N_HEADS, V_DIM)
    pt_flat = page_table.reshape(-1)

    x = x_prompt
    xs = x_sample.reshape(n_s, D_MODEL)
    conv_p, conv_s = [], []
    for l in range(N_A_LAYERS):
        a, st = _mixer_prompt(x, w_in_b[l], conv_w[l])
        conv_p.append(st[:, SUBLANES - 2:, :])
        a_s, u = _mixer_dec(xs, state_conv[l, :, 0, :], state_conv[l, :, 1, :], w_in_b[l],
                            conv_w[l])
        conv_s.append(jnp.stack([state_conv[l, :, 1, :], u], axis=1))
        x, xs = post(x.reshape(n_p * t_p, D_MODEL), a.reshape(n_p * t_p, D_MODEL), xs, a_s,
                     w_mix_b[l], l)
        x = x.reshape(n_p, t_p, D_MODEL)

    x = x.reshape(n_p * t_p, D_MODEL)
    kt_p, v_p, kb, vt, qt = _qkv_prompt(x, (w_k_b, w_k_b.T, w_v_b, w_vt_b, w_qt_b[0]), tabs_p,
                                         tabs_pt, 256, True, n_p)
    k_p = jnp.transpose(kt_p.reshape(n_p, N_HEADS, 2, HEAD_DIM, t_p), (0, 4, 1, 2, 3))
    k_s, v_s, q_s = _qkv_dec(xs, w_qkv_b, tabs_s, True)
    kn = k_s.astype(BF).reshape(n_s, 1, QK_DIM)
    vn = v_s.astype(BF).astype(F32).reshape(n_s, N_HEADS, V_DIM)

    for l in range(N_A_LAYERS, DEPTH):
        i = l - N_A_LAYERS
        lam_init = 0.8 - 0.6 * math.exp(-0.3 * l)
        if i > 0:
            (qt,) = _qkv_prompt(x, (w_qt_b[i],), tabs_p, tabs_pt, 256, False, n_p)
            (q_s,) = _qkv_dec(xs, w_q_b[i], tabs_s, False)
        o, o_s = _attention(qt, kb, vt, pt_flat, q_s.reshape(n_s, 1, QK_DIM), kn, vn, ckt, cv,
                            lams, subln_g, i, lam_init, n_pages)
        x, xs = post(x, o.reshape(n_p * t_p, VO_DIM), xs, o_s.reshape(n_s, VO_DIM).astype(BF),
                     w_o_b[i], l)
    y_prompt = x.reshape(n_p, t_p, D_MODEL)

    return (y_prompt,
            xs.reshape(n_s, 1, D_MODEL),
            k_p,
            v_p.reshape(n_p, t_p, N_HEADS, V_DIM),
            jnp.stack(conv_p),
            k_s.reshape(n_s, 1, N_HEADS, 2, HEAD_DIM),
            v_s.reshape(n_s, 1, N_HEADS, V_DIM),
            jnp.stack(conv_s))
```
